```python
import math
import jax, jax.numpy as jnp
from jax import lax
import numpy as np

D_MODEL = 2048
BATCH = 1
SEQ = 8192
DEPTH = 2
DEC_BATCH = 32
DEC_SEQ = 4
PAST_LEN = 8192
PAGE_SIZE = 128

N_A = DEPTH // 2
N_B = DEPTH - N_A
RWKV_HEAD = 64
RWKV_HEADS = D_MODEL // RWKV_HEAD
DECAY_LORA = max(32, int(round(1.8 * D_MODEL ** 0.5 / 32)) * 32)
ICLR_LORA = DECAY_LORA
N_MIX = 6
DIFF_HEAD = 64
DIFF_HEADS = D_MODEL // (2 * DIFF_HEAD)
DIFF_VDIM = 2 * DIFF_HEAD
Q_DIM = DIFF_HEADS * 2 * DIFF_HEAD
V_DIM = DIFF_HEADS * DIFF_VDIM
Q_BLOCK = 128
ATTN_SCALE = DIFF_HEAD ** -0.5
RMS_EPS = 1e-6
LNX_EPS = 64e-5
NEG_INF = -1e30

kernel_name = 'yoco_rwkv7_diffattn_step'


def _rms(x, g):
    xf = x.astype(jnp.float32)
    return (xf * lax.rsqrt(jnp.mean(xf * xf, axis=-1, keepdims=True) + RMS_EPS)).astype(x.dtype) * g


def _adaln(c, w, b, n):
    return jnp.split(jax.nn.silu(c) @ w + b, n, axis=-1)


def _modnorm(x, g, shift, scale):
    return _rms(x, g) * (1 + scale[:, None, :]) + shift[:, None, :]


def _wkv7_scan(r, w, k, v, a, b, s0):
    xs = tuple(jnp.moveaxis(t.astype(jnp.float32), 1, 0) for t in (r, w, k, v, a, b))

    def step(s, inp):
        r_t, w_t, k_t, v_t, a_t, b_t = inp
        sa = jnp.einsum('bhij,bhj->bhi', s, a_t)
        s = s * w_t[:, :, None, :] + sa[..., None] * b_t[:, :, None, :] + v_t[..., None] * k_t[:, :, None, :]
        return s, jnp.einsum('bhij,bhj->bhi', s, r_t)

    s, y = lax.scan(step, s0.astype(jnp.float32), xs)
    return s, jnp.moveaxis(y, 0, 1)


def _rwkv7_layer(x, c, prev_row, s0, mod_w, mod_b, norm_g, mu, w_in, w0, w1, w2, a0, a1, a2,
                 k_k, k_a, r_k, lnx_w, lnx_b, w_out):
    B, T, D = x.shape
    heads = lambda t: t.reshape(B, T, RWKV_HEADS, RWKV_HEAD)
    shift, scale, gate = _adaln(c, mod_w, mod_b, 3)
    h = _modnorm(x, norm_g, shift, scale)
    dx = jnp.concatenate([prev_row[:, None, :].astype(h.dtype), h[:, :-1]], axis=1) - h
    hm = h[:, :, None, :] + dx[:, :, None, :] * mu
    rkvg = jnp.einsum('btgd,dge->btge', hm[:, :, :4], w_in)
    r, k, v, g = rkvg[:, :, 0], rkvg[:, :, 1], rkvg[:, :, 2], rkvg[:, :, 3]
    xw, xa = hm[:, :, 4], hm[:, :, 5]
    w_log = -jax.nn.softplus(-(w0 + jnp.tanh(xw @ w1) @ w2).astype(jnp.float32)) - 0.5
    decay = jnp.exp(-jnp.exp(w_log))
    a = jax.nn.sigmoid(a0 + (xa @ a1) @ a2).astype(jnp.float32)
    kk = heads((k * k_k).astype(jnp.float32))
    kk = kk / jnp.maximum(jnp.sqrt(jnp.sum(kk * kk, axis=-1, keepdims=True)), 1e-12)
    k = k.astype(jnp.float32) * (1 + (a - 1) * k_a)
    s_fin, y = _wkv7_scan(heads(r), heads(decay), heads(k), heads(v), -kk, kk * heads(a), s0)
    mean = jnp.mean(y, axis=-1, keepdims=True)
    var = jnp.mean(jnp.square(y - mean), axis=-1, keepdims=True)
    y = ((y - mean) * lax.rsqrt(var + LNX_EPS)).reshape(B, T, D) * lnx_w + lnx_b
    bonus = jnp.sum(heads(r) * heads(k) * r_k, axis=-1, keepdims=True) * heads(v)
    o = (y + bonus.reshape(B, T, D)) * jax.nn.silu(g.astype(jnp.float32))
    out = o.astype(x.dtype) @ w_out
    return (x + gate[:, None, :] * out).astype(x.dtype), s_fin, h[:, -1]


def _shared_kv(x, c, mod_w, mod_b, norm_g, w_kv, k_gain):
    B, T, _ = x.shape
    shift, scale = _adaln(c, mod_w, mod_b, 2)
    kv = _modnorm(x, norm_g, shift, scale) @ w_kv
    k = _rms(kv[..., :Q_DIM].reshape(B, T, DIFF_HEADS, 2, DIFF_HEAD), k_gain)
    return (k.reshape(B, T, DIFF_HEADS, 2 * DIFF_HEAD),
            kv[..., Q_DIM:].reshape(B, T, DIFF_HEADS, DIFF_VDIM))


def _diff_queries(x, c, mod_w, mod_b, norm_g, w_in, q_gain):
    B, T, _ = x.shape
    shift, scale, gate = _adaln(c, mod_w, mod_b, 3)
    qz = _modnorm(x, norm_g, shift, scale) @ w_in
    q = _rms(qz[..., :Q_DIM].reshape(B, T, DIFF_HEADS, 2, DIFF_HEAD), q_gain)
    return q, qz[..., Q_DIM:], gate


def _alibi_bias(q_pos, k_pos):
    slopes = jnp.exp2(-8.0 * jnp.arange(1, DIFF_HEADS + 1, dtype=jnp.float32) / DIFF_HEADS)
    dist = (q_pos[:, None] - k_pos[None, :]).astype(jnp.float32)
    return jnp.where(dist >= 0, -slopes[:, None, None] * dist, NEG_INF)


def _partial(q, k, v, bias):
    s = jnp.einsum('bqhd,bkhd->bhqk', q, k).astype(jnp.float32) * ATTN_SCALE + bias
    m = jnp.max(s, axis=-1)
    p = jnp.exp(s - m[..., None])
    return m, jnp.sum(p, axis=-1), jnp.einsum('bhqk,bkhd->bhqd', p, v.astype(jnp.float32))


def _partial_pair(q, k, v, bias):
    return (_partial(q[..., 0, :], k[..., :DIFF_HEAD], v, bias),
            _partial(q[..., 1, :], k[..., DIFF_HEAD:], v, bias))


def _merge(pa, pb):
    (ma, la, aa), (mb, lb, ab) = pa, pb
    m = jnp.maximum(ma, mb)
    ea, eb = jnp.exp(ma - m), jnp.exp(mb - m)
    return m, la * ea + lb * eb, aa * ea[..., None] + ab * eb[..., None]


def _combine(pair, lam):
    (_, l1, acc1), (_, l2, acc2) = pair
    o = acc1 / l1[..., None] - lam * (acc2 / l2[..., None])
    return jnp.transpose(o, (0, 2, 1, 3))


def _diff_attn_prompt(q, k, v, lam):
    B, T = q.shape[:2]
    nb = T // Q_BLOCK
    qb = jnp.moveaxis(q.reshape(B, nb, Q_BLOCK, DIFF_HEADS, 2, DIFF_HEAD), 1, 0)
    k_pos = jnp.arange(T, dtype=jnp.int32)

    def block(args):
        i, qi = args
        bias = _alibi_bias(i * Q_BLOCK + jnp.arange(Q_BLOCK, dtype=jnp.int32), k_pos)
        return _combine(_partial_pair(qi, k, v, bias), lam)

    o = lax.map(block, (jnp.arange(nb, dtype=jnp.int32), qb))
    return jnp.moveaxis(o, 0, 1).reshape(B, T, DIFF_HEADS, DIFF_VDIM)


def _diff_attn_paged(q, k_new, v_new, cache_k, cache_v, page_table, lam):
    t_new = q.shape[1]
    page = cache_k.shape[1]
    n_pages = page_table.shape[1]
    q_pos = n_pages * page + jnp.arange(t_new, dtype=jnp.int32)
    init = _partial_pair(q, k_new, v_new, _alibi_bias(q_pos, q_pos))

    def step(carry, args):
        j, phys = args
        bias = _alibi_bias(q_pos, j * page + jnp.arange(page, dtype=jnp.int32))
        new = _partial_pair(q, cache_k[phys], cache_v[phys], bias)
        return (_merge(carry[0], new[0]), _merge(carry[1], new[1])), None

    acc, _ = lax.scan(step, init, (jnp.arange(n_pages, dtype=jnp.int32), page_table.T))
    return _combine(acc, lam)


def _diff_out(x, o, gate_z, gate, subln_g, lam_init, w_out):
    B, T, _ = x.shape
    o = _rms(o, subln_g.astype(jnp.float32)) * (1.0 - lam_init)
    y = (o.reshape(B, T, V_DIM) * jax.nn.silu(gate_z.astype(jnp.float32))).astype(x.dtype) @ w_out
    return (x + gate[:, None, :] * y).astype(x.dtype)


def setup_inputs(seed: int = 0) -> dict:
    key = jax.random.key(seed)
    keys = jax.random.split(key, 64)
    cnt = [0]
    f32 = jnp.float32
    D = D_MODEL

    def nxt():
        cnt[0] += 1
        return keys[cnt[0] - 1]

    def nrm(shape, scale=1.0):
        return jax.random.normal(nxt(), shape, f32) * scale

    def gain(shape):
        return 1.0 + nrm(shape, 0.05)

    n_pages = PAST_LEN // PAGE_SIZE
    n_used = DEC_BATCH * n_pages
    n_pool = n_used + max(1, n_used // 4)
    inp = {}
    inp['x_prompt'] = nrm((BATCH, SEQ, D))
    inp['x_sample'] = nrm((DEC_BATCH, DEC_SEQ, D))
    inp['c_prompt'] = nrm((BATCH, D))
    inp['c_sample'] = nrm((DEC_BATCH, D))
    inp['state_wkv'] = nrm((N_A, DEC_BATCH, RWKV_HEADS, RWKV_HEAD, RWKV_HEAD), 0.3)
    inp['state_shift'] = nrm((N_A, DEC_BATCH, D))
    inp['cache_k'] = nrm((n_pool, PAGE_SIZE, DIFF_HEADS, 2 * DIFF_HEAD))
    inp['cache_v'] = nrm((n_pool, PAGE_SIZE, DIFF_HEADS, DIFF_VDIM))
    inp['page_table'] = jax.random.permutation(nxt(), n_pool)[:n_used].reshape(DEC_BATCH, n_pages).astype(jnp.int32)
    inp['a_mod_w'] = nrm((N_A, D, 3 * D), 0.5 * D ** -0.5)
    inp['a_mod_b'] = nrm((N_A, 3 * D), 0.02)
    inp['a_norm'] = gain((N_A, D))
    inp['a_mu'] = jax.random.uniform(nxt(), (N_A, N_MIX, D), f32)
    inp['a_w_in'] = nrm((N_A, D, 4, D), D ** -0.5)
    inp['a_w0'] = jax.random.uniform(nxt(), (N_A, D), f32, minval=-4.0, maxval=0.0)
    inp['a_w1'] = nrm((N_A, D, DECAY_LORA), D ** -0.5)
    inp['a_w2'] = nrm((N_A, DECAY_LORA, D), 0.1 * DECAY_LORA ** -0.5)
    inp['a_a0'] = nrm((N_A, D), 0.1)
    inp['a_a1'] = nrm((N_A, D, ICLR_LORA), D ** -0.5)
    inp['a_a2'] = nrm((N_A, ICLR_LORA, D), 0.1 * ICLR_LORA ** -0.5)
    inp['a_k_k'] = 0.85 + nrm((N_A, D), 0.05)
    inp['a_k_a'] = gain((N_A, D))
    inp['a_r_k'] = nrm((N_A, RWKV_HEADS, RWKV_HEAD), 0.1)
    inp['a_lnx_w'] = gain((N_A, D))
    inp['a_lnx_b'] = nrm((N_A, D), 0.02)
    inp['a_w_out'] = nrm((N_A, D, D), D ** -0.5)
    inp['kv_mod_w'] = nrm((D, 2 * D), 0.5 * D ** -0.5)
    inp['kv_mod_b'] = nrm((2 * D,), 0.02)
    inp['kv_norm'] = gain((D,))
    inp['kv_w'] = nrm((D, Q_DIM + V_DIM), D ** -0.5)
    inp['kv_k_norm'] = gain((2, DIFF_HEAD))
    inp['b_mod_w'] = nrm((N_B, D, 3 * D), 0.5 * D ** -0.5)
    inp['b_mod_b'] = nrm((N_B, 3 * D), 0.02)
    inp['b_norm'] = gain((N_B, D))
    inp['b_w_in'] = nrm((N_B, D, Q_DIM + V_DIM), D ** -0.5)
    inp['b_q_norm'] = gain((N_B, 2, DIFF_HEAD))
    inp['b_lam'] = nrm((N_B, 4, DIFF_HEAD), 0.1)
    inp['b_subln'] = gain((N_B, DIFF_VDIM))
    inp['b_w_out'] = nrm((N_B, V_DIM, D), V_DIM ** -0.5)
    return inp


def reference(x_prompt, x_sample, c_prompt, c_sample, state_wkv, state_shift, cache_k, cache_v, page_table,
              a_mod_w, a_mod_b, a_norm, a_mu, a_w_in, a_w0, a_w1, a_w2, a_a0, a_a1, a_a2,
              a_k_k, a_k_a, a_r_k, a_lnx_w, a_lnx_b, a_w_out,
              kv_mod_w, kv_mod_b, kv_norm, kv_w, kv_k_norm,
              b_mod_w, b_mod_b, b_norm, b_w_in, b_q_norm, b_lam, b_subln, b_w_out):
    xp, xs = x_prompt, x_sample
    bp = xp.shape[0]
    wkv_p, wkv_s, shift_p, shift_s = [], [], [], []
    k_p = v_p = k_s = v_s = None
    for layer in range(DEPTH):
        if layer < N_A:
            prm = (a_mod_w[layer], a_mod_b[layer], a_norm[layer], a_mu[layer], a_w_in[layer],
                   a_w0[layer], a_w1[layer], a_w2[layer], a_a0[layer], a_a1[layer], a_a2[layer],
                   a_k_k[layer], a_k_a[layer], a_r_k[layer], a_lnx_w[layer], a_lnx_b[layer], a_w_out[layer])
            xp, s_fin, last = _rwkv7_layer(
                xp, c_prompt, jnp.zeros((bp, D_MODEL), xp.dtype),
                jnp.zeros((bp, RWKV_HEADS, RWKV_HEAD, RWKV_HEAD), jnp.float32), *prm)
            wkv_p.append(s_fin.astype(state_wkv.dtype))
            shift_p.append(last.astype(state_shift.dtype))
            xs, s_fin, last = _rwkv7_layer(xs, c_sample, state_shift[layer], state_wkv[layer], *prm)
            wkv_s.append(s_fin.astype(state_wkv.dtype))
            shift_s.append(last.astype(state_shift.dtype))
        else:
            if layer == N_A:
                kvp = (kv_mod_w, kv_mod_b, kv_norm, kv_w, kv_k_norm)
                k_p, v_p = _shared_kv(xp, c_prompt, *kvp)
                k_s, v_s = _shared_kv(xs, c_sample, *kvp)
            i = layer - N_A
            lam_init = 0.8 - 0.6 * math.exp(-0.3 * layer)
            lq = b_lam[i].astype(jnp.float32)
            lam = jnp.exp(jnp.sum(lq[0] * lq[1])) - jnp.exp(jnp.sum(lq[2] * lq[3])) + lam_init
            qprm = (b_mod_w[i], b_mod_b[i], b_norm[i], b_w_in[i], b_q_norm[i])
            q, gz, gt = _diff_queries(xp, c_prompt, *qprm)
            xp = _diff_out(xp, _diff_attn_prompt(q, k_p, v_p, lam), gz, gt, b_subln[i], lam_init, b_w_out[i])
            q, gz, gt = _diff_queries(xs, c_sample, *qprm)
            o = _diff_attn_paged(q, k_s, v_s, cache_k, cache_v, page_table, lam)
            xs = _diff_out(xs, o, gz, gt, b_subln[i], lam_init, b_w_out[i])
    wkv_prompt = jnp.stack(wkv_p)
    shift_prompt = jnp.stack(shift_p)
    wkv_sample = jnp.stack(wkv_s)
    shift_sample = jnp.stack(shift_s)
    return (xp, xs, wkv_prompt, shift_prompt, k_p, v_p, wkv_sample, shift_sample, k_s, v_s)
```

```python
import functools
import math

import jax
import jax.numpy as jnp
from jax import lax
from jax.experimental import pallas as pl
from jax.experimental.pallas import tpu as pltpu

F32 = jnp.float32
BF16 = jnp.bfloat16
HIGHEST = lax.Precision.HIGHEST

LANES = 128
RWKV_HEAD = 64
DIFF_HEAD = 64
PAGE = 128
RMS_EPS = 1e-6
LNX_EPS = 64e-5
NEG_INF = -1e30
ATTN_SCALE = DIFF_HEAD ** -0.5
WKV_CHUNK = 64
VMEM_LIMIT = 56 * 1024 * 1024

_NT = (((1,), (1,)), ((), ()))
_TN = (((0,), (0,)), ((), ()))


def _dot(a, b):
    return jnp.dot(a.astype(BF16), b.astype(BF16), preferred_element_type=F32)


def _dot_nt(a, b):
    return lax.dot_general(a.astype(BF16), b.astype(BF16), _NT, preferred_element_type=F32)


def _dot_tn(a, b):
    return lax.dot_general(a.astype(BF16), b.astype(BF16), _TN, preferred_element_type=F32)


def _silu(x):
    return x * (1.0 / (1.0 + jnp.exp(-x)))


def _col_tile(n):
    return next(t for t in (512, 384, 256, 128) if n % t == 0)


def _params(sem):
    return pltpu.CompilerParams(dimension_semantics=sem, vmem_limit_bytes=VMEM_LIMIT)


def _half_mask():
    return lax.broadcasted_iota(jnp.int32, (1, LANES), 1) < 64


def _group64_sum(x, m0):
    s0 = jnp.sum(jnp.where(m0, x, 0.0), axis=-1, keepdims=True)
    s1 = jnp.sum(jnp.where(m0, 0.0, x), axis=-1, keepdims=True)
    return jnp.where(m0, s0, s1)


def _mod_body(c_ref, w_ref, b_ref, o_ref):
    o_ref[...] = jnp.dot(_silu(c_ref[...]), w_ref[...], precision=HIGHEST,
                         preferred_element_type=F32) + b_ref[...]


def _mod_vectors(c, w, b):
    m, d = c.shape
    n = w.shape[1]
    tn = _col_tile(n)
    return pl.pallas_call(
        _mod_body,
        grid=(n // tn,),
        in_specs=[pl.BlockSpec((m, d), lambda j: (0, 0)),
                  pl.BlockSpec((d, tn), lambda j: (0, j)),
                  pl.BlockSpec((1, tn), lambda j: (0, j))],
        out_specs=pl.BlockSpec((m, tn), lambda j: (0, j)),
        out_shape=jax.ShapeDtypeStruct((m, n), F32),
        compiler_params=_params(("arbitrary",)),
        name="mod_vectors",
    )(c, w, b.reshape(1, n))


def _modnorm_val(x, g, shift, scale):
    r = lax.rsqrt(jnp.mean(x * x, axis=-1, keepdims=True) + RMS_EPS)
    return (x * r) * g * (1.0 + scale) + shift


def _modnorm_body(x_ref, g_ref, sh_ref, sc_ref, o_ref):
    o_ref[...] = _modnorm_val(x_ref[...], g_ref[...], sh_ref[...], sc_ref[...])


def _modnorm(x, g, shift, scale, tm):
    m, d = x.shape
    rb = shift.shape[0]
    return pl.pallas_call(
        _modnorm_body,
        grid=(m // tm,),
        in_specs=[pl.BlockSpec((tm, d), lambda i: (i, 0)),
                  pl.BlockSpec((1, d), lambda i: (0, 0)),
                  pl.BlockSpec((rb, d), lambda i: (0, 0)),
                  pl.BlockSpec((rb, d), lambda i: (0, 0))],
        out_specs=pl.BlockSpec((tm, d), lambda i: (i, 0)),
        out_shape=jax.ShapeDtypeStruct((m, d), F32),
        compiler_params=_params(("arbitrary",)),
        name="modnorm",
    )(x, g.reshape(1, d), shift, scale)


def _fused_linear(name, prologue, epilogue, prow, pbc, w, erow, ebc, out_dtypes, *, tm, tn, groups=1):
    prow = [a if isinstance(a, tuple) else (a, 0) for a in prow]
    prow_blk = [cb for _, cb in prow]
    prow = [a for a, _ in prow]
    m = prow[0].shape[0]
    k, n = w.shape
    nj = n // tn
    njg = nj // groups
    n_prow, n_pbc, n_erow, n_ebc, n_out = len(prow), len(pbc), len(erow), len(ebc), len(out_dtypes)

    def body(*refs):
        it = iter(refs)
        prow_r = [next(it) for _ in range(n_prow)]
        pbc_r = [next(it) for _ in range(n_pbc)]
        w_r = next(it)
        erow_r = [next(it) for _ in range(n_erow)]
        ebc_r = [next(it) for _ in range(n_ebc)]
        out_r = [next(it) for _ in range(n_out)]
        xs = next(it)
        j = pl.program_id(1)

        @pl.when(j % njg == 0)
        def _():
            xs[...] = prologue(prow_r, pbc_r, j // njg).astype(BF16)

        acc = jnp.dot(xs[...], w_r[...], preferred_element_type=F32)
        res = epilogue(acc, erow_r, ebc_r)
        for o, r in zip(out_r, res):
            o[...] = r.astype(o.dtype)

    in_specs = ([pl.BlockSpec((tm, k), lambda i, j, cb=cb: (i, cb)) for cb in prow_blk]
                + [pl.BlockSpec(a.shape, lambda i, j: (0, 0)) for a in pbc]
                + [pl.BlockSpec((k, tn), lambda i, j: (0, j))]
                + [pl.BlockSpec((tm, tn), lambda i, j: (i, j)) for _ in erow]
                + [pl.BlockSpec((a.shape[0], tn), lambda i, j: (0, j)) for a in ebc])
    outs = pl.pallas_call(
        body,
        grid=(m // tm, nj),
        in_specs=in_specs,
        out_specs=[pl.BlockSpec((tm, tn), lambda i, j: (i, j)) for _ in out_dtypes],
        out_shape=[jax.ShapeDtypeStruct((m, n), dt) for dt in out_dtypes],
        scratch_shapes=[pltpu.VMEM((tm, k), BF16)],
        compiler_params=_params(("arbitrary", "arbitrary")),
        name=name,
    )(*prow, *pbc, w, *erow, *ebc)
    return outs


def _plain_epilogue(acc, erow_r, ebc_r):
    return (acc,)


def _residual_epilogue(acc, erow_r, ebc_r):
    return (erow_r[0][...] + ebc_r[0][...] * acc,)


def _group_rms_epilogue(with_bf16):
    def epilogue(acc, erow_r, ebc_r):
        m0 = _half_mask()
        gain = ebc_r[0][...]
        cols = []
        for s in range(acc.shape[1] // LANES):
            x = acc[:, s * LANES:(s + 1) * LANES]
            ms = _group64_sum(x * x, m0) * (1.0 / 64.0)
            cols.append(x * lax.rsqrt(ms + RMS_EPS) * gain[:, s * LANES:(s + 1) * LANES])
        y = jnp.concatenate(cols, axis=1) if len(cols) > 1 else cols[0]
        return (y, y) if with_bf16 else (y,)
    return epilogue


def _modnorm_prologue(prow_r, pbc_r, g):
    return _modnorm_val(prow_r[0][...], pbc_r[0][...], pbc_r[1][...], pbc_r[2][...])


def _mix_prologue(prow_r, pbc_r, g):
    h = prow_r[0][...]
    mu = pbc_r[0][pl.ds(g, 1), :]
    return h + (prow_r[1][...] - h) * mu


def _gated_prologue(prow_r, pbc_r, g):
    return prow_r[0][...] * _silu(prow_r[1][...])


def _lora_body(h_ref, hp_ref, mu_ref, w0_ref, w1_ref, w2_ref, a0_ref, a1_ref, a2_ref, lw_ref, a_ref):
    h = h_ref[...]
    dx = hp_ref[...] - h
    xw = h + dx * mu_ref[4:5, :]
    xa = h + dx * mu_ref[5:6, :]
    wl = w0_ref[...] + _dot(jnp.tanh(_dot(xw, w1_ref[...])), w2_ref[...])
    z = -wl
    softplus = jnp.maximum(z, 0.0) + jnp.log(1.0 + jnp.exp(-jnp.abs(z)))
    lw_ref[...] = -jnp.exp(-softplus - 0.5)
    al = a0_ref[...] + _dot(_dot(xa, a1_ref[...]), a2_ref[...])
    a_ref[...] = 1.0 / (1.0 + jnp.exp(-al))


def _lora(h, hprev, mu, w0, w1, w2, a0, a1, a2, tm):
    m, d = h.shape
    lp = w1.shape[1]
    row = pl.BlockSpec((tm, d), lambda i: (i, 0))
    full = lambda a: pl.BlockSpec(a.shape, lambda i: (0, 0))
    args = (h, hprev, mu, w0.reshape(1, d), w1, w2, a0.reshape(1, d), a1, a2)
    return pl.pallas_call(
        _lora_body,
        grid=(m // tm,),
        in_specs=[row, row] + [full(a) for a in args[2:]],
        out_specs=[row, row],
        out_shape=[jax.ShapeDtypeStruct((m, d), F32)] * 2,
        compiler_params=_params(("arbitrary",)),
        name="rwkv_lora",
    )(*args)


def _wkv_body(r_ref, k_ref, v_ref, lw_ref, a_ref, kk_ref, ka_ref, rk_ref, lnw_ref, lnb_ref, s0_ref,
              o_ref, sf_ref, z_ref, *, chunk, n_chunks):
    c2 = 2 * chunk
    t_blk = pl.program_id(2)
    m0 = _half_mask()

    @pl.when(t_blk == 0)
    def _():
        zeros = jnp.zeros((RWKV_HEAD, RWKV_HEAD), F32)
        top = jnp.concatenate([s0_ref[0, 0], zeros], axis=1)
        bot = jnp.concatenate([zeros, s0_ref[0, 1]], axis=1)
        z_ref[...] = jnp.concatenate([top, bot], axis=0)

    row = lax.broadcasted_iota(jnp.int32, (c2, c2), 0)
    col = lax.broadcasted_iota(jnp.int32, (c2, c2), 1)
    strict = row > col
    incl = row >= col
    eye = (row == col).astype(F32)
    tril_c = (lax.broadcasted_iota(jnp.int32, (chunk, chunk), 0)
              >= lax.broadcasted_iota(jnp.int32, (chunk, chunk), 1)).astype(F32)
    n_double = int(math.log2(chunk)) - 1

    def stack(x):
        return jnp.concatenate([jnp.where(m0, x, 0.0), jnp.where(m0, 0.0, x)], axis=0)

    def step(c, carry):
        sl = pl.ds(pl.multiple_of(c * chunk, chunk), chunk)
        r = r_ref[sl, :]
        kr = k_ref[sl, :]
        v = v_ref[sl, :]
        lw = lw_ref[sl, :]
        asg = a_ref[sl, :]

        kk = kr * kk_ref[...]
        kk = kk / jnp.maximum(jnp.sqrt(_group64_sum(kk * kk, m0)), 1e-12)
        k = kr * (1.0 + (asg - 1.0) * ka_ref[...])
        b = kk * asg

        cum = jnp.dot(tril_c, lw, precision=HIGHEST, preferred_element_type=F32)
        cum_end = cum[chunk - 1:chunk, :]
        e_out = jnp.exp(cum_end - cum)
        e_in = jnp.exp(cum - cum_end)
        a_hat = -kk * jnp.exp(cum - lw - cum_end)
        r_hat = r * e_in
        b_hat = b * e_out
        k_hat = k * e_out

        ar = jnp.concatenate([stack(a_hat), stack(r_hat)], axis=0)
        bk = jnp.concatenate([stack(b_hat), stack(k_hat)], axis=0)
        v_st = stack(v)

        mm = _dot_nt(ar, bk)
        l_ab = jnp.where(strict, mm[:c2, :c2], 0.0)
        l_ak = jnp.where(strict, mm[:c2, c2:], 0.0)
        l_rb = jnp.where(incl, mm[c2:, :c2], 0.0)
        l_rk = jnp.where(incl, mm[c2:, c2:], 0.0)

        t_inv = eye + l_ab
        l_pow = l_ab
        for _ in range(n_double):
            l_pow = _dot(l_pow, l_pow)
            t_inv = t_inv + _dot(l_pow, t_inv)

        zd = z_ref[...] * jnp.exp(cum_end)
        p = _dot_nt(ar, zd)
        u = _dot(t_inv, p[:c2] + _dot(l_ak, v_st))
        y_st = p[c2:] + _dot(l_rb, u) + _dot(l_rk, v_st)
        z_ref[...] = zd + _dot_tn(jnp.concatenate([u, v_st], axis=0), bk)
        y = y_st[:chunk] + y_st[chunk:]

        mean = _group64_sum(y, m0) * (1.0 / 64.0)
        yc = y - mean
        var = _group64_sum(yc * yc, m0) * (1.0 / 64.0)
        bonus = _group64_sum(r * k * rk_ref[...], m0) * v
        o_ref[sl, :] = yc * lax.rsqrt(var + LNX_EPS) * lnw_ref[...] + lnb_ref[...] + bonus
        return carry

    lax.fori_loop(0, n_chunks, step, 0)

    @pl.when(t_blk == pl.num_programs(2) - 1)
    def _():
        z = z_ref[...]
        sf_ref[0, 0] = z[:RWKV_HEAD, :RWKV_HEAD]
        sf_ref[0, 1] = z[RWKV_HEAD:, RWKV_HEAD:]


def _wkv(rkvg, lw, asig, k_k, k_a, r_k, lnx_w, lnx_b, s0, *, batch, t_len, t_blk):
    m, d = lw.shape
    hp = d // LANES
    nt = t_len // t_blk
    heads = d // RWKV_HEAD
    vec = lambda a: a.reshape(1, d)
    row = lambda off: pl.BlockSpec((t_blk, LANES), lambda b, p, t, off=off: (b * nt + t, off + p))
    par = pl.BlockSpec((1, LANES), lambda b, p, t: (0, p))
    st = pl.BlockSpec((1, 2, RWKV_HEAD, RWKV_HEAD), lambda b, p, t: (b, p, 0, 0))
    body = functools.partial(_wkv_body, chunk=WKV_CHUNK, n_chunks=t_blk // WKV_CHUNK)
    return pl.pallas_call(
        body,
        grid=(batch, hp, nt),
        in_specs=[row(0), row(hp), row(2 * hp), row(0), row(0), par, par, par, par, par, st],
        out_specs=[row(0), st],
        out_shape=[jax.ShapeDtypeStruct((m, d), F32),
                   jax.ShapeDtypeStruct((batch, heads, RWKV_HEAD, RWKV_HEAD), F32)],
        scratch_shapes=[pltpu.VMEM((LANES, LANES), F32)],
        compiler_params=_params(("arbitrary", "arbitrary", "arbitrary")),
        name="wkv7_chunked",
    )(rkvg, rkvg, rkvg, lw, asig, vec(k_k), vec(k_a), vec(r_k), vec(lnx_w), vec(lnx_b), s0)


def _lambda_val(lam_ref, lam_init):
    lq = lam_ref[...]
    s1 = jnp.sum(lq[0:1] * lq[1:2], axis=-1, keepdims=True)
    s2 = jnp.sum(lq[2:3] * lq[3:4], axis=-1, keepdims=True)
    return jnp.exp(s1) - jnp.exp(s2) + lam_init


def _sub_ln(o, g, lam_init):
    return o * lax.rsqrt(jnp.mean(o * o, axis=-1, keepdims=True) + RMS_EPS) * g * (1.0 - lam_init)


def _head_slope(h, heads):
    return jnp.exp2((-8.0 / heads) * (h + 1).astype(F32))


def _attn_body(q_ref, k_ref, v_ref, lam_ref, g_ref, o_ref, q2_ref, m_ref, l_ref, acc_ref, *,
               tq, heads, lam_init):
    h = pl.program_id(0)
    i = pl.program_id(1)
    m0 = _half_mask()
    slope = _head_slope(jnp.full((1, 1), h, jnp.int32), heads)

    q = q_ref[...].astype(F32) * ATTN_SCALE
    q2_ref[...] = jnp.concatenate([jnp.where(m0, q, 0.0), jnp.where(m0, 0.0, q)], axis=0).astype(BF16)
    m_ref[...] = jnp.full(m_ref.shape, NEG_INF, F32)
    l_ref[...] = jnp.zeros(l_ref.shape, F32)
    acc_ref[...] = jnp.zeros(acc_ref.shape, F32)

    col = lax.broadcasted_iota(jnp.int32, (1, tq), 1)

    def block(j, masked):
        sl = pl.ds(pl.multiple_of(j * tq, tq), tq)
        s = lax.dot_general(q2_ref[...], k_ref[sl, :], _NT, preferred_element_type=F32)
        s = s + slope * ((j - i) * tq + col).astype(F32)
        if masked:
            rr = lax.broadcasted_iota(jnp.int32, (2 * tq, tq), 0) & (tq - 1)
            cc = lax.broadcasted_iota(jnp.int32, (2 * tq, tq), 1)
            s = jnp.where(cc <= rr, s, NEG_INF)
        m_prev = m_ref[...]
        m_next = jnp.maximum(m_prev, jnp.max(s, axis=-1, keepdims=True))
        p = jnp.exp(s - m_next[:, :1])
        alpha = jnp.exp(m_prev - m_next)
        l_ref[...] = alpha * l_ref[...] + jnp.sum(p, axis=-1, keepdims=True)
        acc_ref[...] = alpha * acc_ref[...] + jnp.dot(p.astype(BF16), v_ref[sl, :],
                                                      preferred_element_type=F32)
        m_ref[...] = m_next

    def full_block(j, carry):
        block(j, False)
        return carry

    lax.fori_loop(0, i, full_block, 0)
    block(i, True)

    lam = _lambda_val(lam_ref, lam_init)
    inv_l = 1.0 / l_ref[...]
    acc = acc_ref[...] * inv_l
    o = acc[:tq] - lam * acc[tq:]
    o_ref[...] = _sub_ln(o, g_ref[...], lam_init)


def _attn_prompt(q, k, v, lam, subln, lam_init, tq):
    t, hd = q.shape
    heads = hd // LANES
    body = functools.partial(_attn_body, tq=tq, heads=heads, lam_init=lam_init)
    return pl.pallas_call(
        body,
        grid=(heads, t // tq),
        in_specs=[pl.BlockSpec((tq, LANES), lambda h, i: (i, h)),
                  pl.BlockSpec((t, LANES), lambda h, i: (0, h)),
                  pl.BlockSpec((t, LANES), lambda h, i: (0, h)),
                  pl.BlockSpec(lam.shape, lambda h, i: (0, 0)),
                  pl.BlockSpec((1, LANES), lambda h, i: (0, 0))],
        out_specs=pl.BlockSpec((tq, LANES), lambda h, i: (i, h)),
        out_shape=jax.ShapeDtypeStruct((t, hd), F32),
        scratch_shapes=[pltpu.VMEM((2 * tq, LANES), BF16),
                        pltpu.VMEM((2 * tq, LANES), F32),
                        pltpu.VMEM((2 * tq, LANES), F32),
                        pltpu.VMEM((2 * tq, LANES), F32)],
        compiler_params=_params(("arbitrary", "arbitrary")),
        name="diff_attn_prompt",
    )(q, k, v, lam, subln.reshape(1, LANES))


TSLOT = 8


def _paged_body(pt_ref, q_ref, ck_ref, cv_ref, kn_ref, vn_ref, lam_ref, g_ref, o_ref,
                q2_ref, m_ref, l_ref, acc_ref, *, heads, n_pages, t_new, lam_init):
    j = pl.program_id(1)
    rows = heads * 2 * TSLOT
    m0 = _half_mask()

    ridx = lax.broadcasted_iota(jnp.int32, (rows, 1), 0)
    slope = _head_slope(ridx >> int(math.log2(2 * TSLOT)), heads)
    tpos = ridx & (TSLOT - 1)
    col = lax.broadcasted_iota(jnp.int32, (1, PAGE), 1)

    @pl.when(j == 0)
    def _():
        q = q_ref[0].astype(F32) * ATTN_SCALE
        for h in range(heads):
            qh = q[:, h * LANES:(h + 1) * LANES]
            q2_ref[h * 2 * TSLOT:(h + 1) * 2 * TSLOT, :] = jnp.concatenate(
                [jnp.where(m0, qh, 0.0), jnp.where(m0, 0.0, qh)], axis=0).astype(BF16)
        m_ref[...] = jnp.full(m_ref.shape, NEG_INF, F32)
        l_ref[...] = jnp.zeros(l_ref.shape, F32)
        acc_ref[...] = jnp.zeros(acc_ref.shape, F32)

    def attend(k_page, v_page, bias):
        s = jnp.concatenate(
            [lax.dot_general(q2_ref[h * 2 * TSLOT:(h + 1) * 2 * TSLOT, :],
                             k_page[:, h * LANES:(h + 1) * LANES].astype(BF16), _NT,
                             preferred_element_type=F32) for h in range(heads)], axis=0) + bias
        m_prev = m_ref[...]
        m_next = jnp.maximum(m_prev, jnp.max(s, axis=-1, keepdims=True))
        p = jnp.exp(s - m_next[:, :1])
        alpha = jnp.exp(m_prev - m_next)
        l_ref[...] = alpha * l_ref[...] + jnp.sum(p, axis=-1, keepdims=True)
        p = p.astype(BF16)
        pv = jnp.concatenate(
            [jnp.dot(p[h * 2 * TSLOT:(h + 1) * 2 * TSLOT, :],
                     v_page[:, h * LANES:(h + 1) * LANES].astype(BF16),
                     preferred_element_type=F32) for h in range(heads)], axis=0)
        acc_ref[...] = alpha * acc_ref[...] + pv
        m_ref[...] = m_next

    @pl.when(j < n_pages)
    def _():
        dist = (n_pages * PAGE + tpos - j * PAGE - col).astype(F32)
        attend(ck_ref.at[0], cv_ref.at[0], -slope * dist)

    @pl.when(j == n_pages)
    def _():
        valid = (col <= tpos) & (col < t_new)
        bias = jnp.where(valid, -slope * (tpos - col).astype(F32), NEG_INF)
        attend(kn_ref.at[0], vn_ref.at[0], bias)
        lam = _lambda_val(lam_ref, lam_init)
        acc = acc_ref[...] * (1.0 / l_ref[...])
        for h in range(heads):
            base = h * 2 * TSLOT
            o = acc[base:base + TSLOT] - lam * acc[base + TSLOT:base + 2 * TSLOT]
            o_ref[0, :, h * LANES:(h + 1) * LANES] = _sub_ln(o, g_ref[...], lam_init)


def _attn_paged(q, k_new, v_new, cache_k, cache_v, page_table, lam, subln, lam_init, t_new):
    bsz, _, hd = q.shape
    heads = hd // LANES
    n_pages = page_table.shape[1]
    rows = heads * 2 * TSLOT
    body = functools.partial(_paged_body, heads=heads, n_pages=n_pages, t_new=t_new, lam_init=lam_init)
    page = lambda b, j, pt: (pt[b, jnp.minimum(j, n_pages - 1)], 0, 0)
    per_b = lambda b, j, pt: (b, 0, 0)
    grid_spec = pltpu.PrefetchScalarGridSpec(
        num_scalar_prefetch=1,
        grid=(bsz, n_pages + 1),
        in_specs=[pl.BlockSpec((1, TSLOT, hd), per_b),
                  pl.BlockSpec((1, PAGE, hd), page),
                  pl.BlockSpec((1, PAGE, hd), page),
                  pl.BlockSpec((1, PAGE, hd), per_b),
                  pl.BlockSpec((1, PAGE, hd), per_b),
                  pl.BlockSpec(lam.shape, lambda b, j, pt: (0, 0)),
                  pl.BlockSpec((1, LANES), lambda b, j, pt: (0, 0))],
        out_specs=pl.BlockSpec((1, TSLOT, hd), per_b),
        scratch_shapes=[pltpu.VMEM((rows, LANES), BF16),
                        pltpu.VMEM((rows, LANES), F32),
                        pltpu.VMEM((rows, LANES), F32),
                        pltpu.VMEM((rows, LANES), F32)])
    return pl.pallas_call(
        body,
        grid_spec=grid_spec,
        out_shape=jax.ShapeDtypeStruct((bsz, TSLOT, hd), F32),
        compiler_params=_params(("arbitrary", "arbitrary")),
        name="diff_attn_paged",
    )(page_table, q, cache_k, cache_v, k_new, v_new, lam, subln.reshape(1, LANES))


def _row_tile(m):
    return min(512, m)


def _rwkv_layer(x, shift, scale, gate, h_prev_row, s0, p, *, batch, t_len):
    m, d = x.shape
    tm = _row_tile(m)
    tn = _col_tile(d)
    h = _modnorm(x, p["norm"], shift, scale, tm)
    h3 = h.reshape(batch, t_len, d)
    hprev = jnp.concatenate([h_prev_row[:, None, :], h3[:, :-1]], axis=1).reshape(m, d)

    rkvg, = _fused_linear("rwkv_in_proj", _mix_prologue, _plain_epilogue, [h, hprev], [p["mu"]],
                          p["w_in"], [], [], [F32], tm=tm, tn=tn, groups=4)
    lw, asig = _lora(h, hprev, p["mu"], p["w0"], p["w1"], p["w2"], p["a0"], p["a1"], p["a2"], min(256, m))

    t_pad = -(-t_len // WKV_CHUNK) * WKV_CHUNK
    if t_pad != t_len:
        padr = lambda a: jnp.pad(a.reshape(batch, t_len, -1),
                                 ((0, 0), (0, t_pad - t_len), (0, 0))).reshape(batch * t_pad, -1)
        rkvg_w, lw_w, asig_w = padr(rkvg), padr(lw), padr(asig)
    else:
        rkvg_w, lw_w, asig_w = rkvg, lw, asig
    t_blk = min(512, t_pad)
    o_pre, s_fin = _wkv(rkvg_w, lw_w, asig_w, p["k_k"], p["k_a"], p["r_k"], p["lnx_w"], p["lnx_b"], s0,
                        batch=batch, t_len=t_pad, t_blk=t_blk)
    if t_pad != t_len:
        o_pre = o_pre.reshape(batch, t_pad, d)[:, :t_len].reshape(m, d)

    x_new, = _fused_linear("rwkv_out_proj", _gated_prologue, _residual_epilogue, [o_pre, (rkvg, 3)], [],
                           p["w_out"], [x], [gate], [F32], tm=tm, tn=tn)
    return x_new, s_fin, h3[:, -1]


def _tile_gain(gain, n):
    return jnp.tile(gain.reshape(1, LANES), (1, n // LANES))


def _shared_kv(x, shift, scale, p):
    m, d = x.shape
    tm = _row_tile(m)
    qd = p["w_k"].shape[1]
    pbc = [p["norm"].reshape(1, d), shift, scale]
    k, k16 = _fused_linear("kv_k_proj", _modnorm_prologue, _group_rms_epilogue(True), [x], pbc, p["w_k"],
                           [], [_tile_gain(p["k_norm"], qd)], [F32, BF16], tm=tm, tn=_col_tile(qd))
    v, v16 = _fused_linear("kv_v_proj", _modnorm_prologue, lambda acc, e, b: (acc, acc), [x], pbc, p["w_v"],
                           [], [], [F32, BF16], tm=tm, tn=_col_tile(p["w_v"].shape[1]))
    return k, v, k16, v16


def _diff_queries(x, shift, scale, p):
    m, d = x.shape
    tm = _row_tile(m)
    qd = p["w_q"].shape[1]
    pbc = [p["norm"].reshape(1, d), shift, scale]
    q16, = _fused_linear("diff_q_proj", _modnorm_prologue, _group_rms_epilogue(False), [x], pbc, p["w_q"],
                         [], [_tile_gain(p["q_norm"], qd)], [BF16], tm=tm, tn=_col_tile(qd))
    gz, = _fused_linear("diff_gate_proj", _modnorm_prologue, _plain_epilogue, [x], pbc, p["w_z"],
                        [], [], [F32], tm=tm, tn=_col_tile(p["w_z"].shape[1]))
    return q16, gz


def _diff_out(x, o, gz, gate, w_out):
    m, d = x.shape
    y, = _fused_linear("diff_out_proj", _gated_prologue, _residual_epilogue, [o, gz], [], w_out,
                       [x], [gate], [F32], tm=_row_tile(m), tn=_col_tile(d))
    return y


def kernel(x_prompt, x_sample, c_prompt, c_sample, state_wkv, state_shift, cache_k, cache_v, page_table,
           a_mod_w, a_mod_b, a_norm, a_mu, a_w_in, a_w0, a_w1, a_w2, a_a0, a_a1, a_a2,
           a_k_k, a_k_a, a_r_k, a_lnx_w, a_lnx_b, a_w_out,
           kv_mod_w, kv_mod_b, kv_norm, kv_w, kv_k_norm,
           b_mod_w, b_mod_b, b_norm, b_w_in, b_q_norm, b_lam, b_subln, b_w_out):
    bp, t_p, d = x_prompt.shape
    bs, t_s, _ = x_sample.shape
    n_a = a_mod_w.shape[0]
    n_b = b_mod_w.shape[0]
    depth = n_a + n_b
    heads_r = d // RWKV_HEAD
    q_dim = b_w_in.shape[2] - d
    lora = a_w1.shape[2]
    lora_pad = -(-lora // LANES) * LANES

    xp = x_prompt.reshape(bp * t_p, d)
    xs = x_sample.reshape(bs * t_s, d)

    n_c = bp + bs
    c_all = jnp.pad(jnp.concatenate([c_prompt, c_sample], axis=0), ((0, -n_c % 8), (0, 0)))

    def mods(w, b, n):
        mv = _mod_vectors(c_all, w, b)
        parts = jnp.split(mv, n, axis=-1)
        assert bp == 1
        prompt = [q[0:1] for q in parts]
        sample = [jnp.repeat(q[bp:n_c], t_s, axis=0) for q in parts]
        return prompt, sample

    wkv_p, wkv_s, shift_p, shift_s = [], [], [], []
    for layer in range(n_a):
        prm = dict(
            norm=a_norm[layer], mu=a_mu[layer],
            w_in=a_w_in[layer].reshape(d, 4 * d).astype(BF16),
            w0=a_w0[layer], a0=a_a0[layer],
            w1=jnp.pad(a_w1[layer], ((0, 0), (0, lora_pad - lora))).astype(BF16),
            w2=jnp.pad(a_w2[layer], ((0, lora_pad - lora), (0, 0))).astype(BF16),
            a1=jnp.pad(a_a1[layer], ((0, 0), (0, lora_pad - lora))).astype(BF16),
            a2=jnp.pad(a_a2[layer], ((0, lora_pad - lora), (0, 0))).astype(BF16),
            k_k=a_k_k[layer], k_a=a_k_a[layer], r_k=a_r_k[layer],
            lnx_w=a_lnx_w[layer], lnx_b=a_lnx_b[layer],
            w_out=a_w_out[layer].astype(BF16))
        (sh_p, sc_p, gt_p), (sh_s, sc_s, gt_s) = mods(a_mod_w[layer], a_mod_b[layer], 3)
        xp, s_fin, last = _rwkv_layer(
            xp, sh_p, sc_p, gt_p, jnp.zeros((bp, d), F32),
            jnp.zeros((bp, heads_r, RWKV_HEAD, RWKV_HEAD), F32), prm, batch=bp, t_len=t_p)
        wkv_p.append(s_fin)
        shift_p.append(last)
        xs, s_fin, last = _rwkv_layer(
            xs, sh_s, sc_s, gt_s, state_shift[layer], state_wkv[layer], prm, batch=bs, t_len=t_s)
        wkv_s.append(s_fin)
        shift_s.append(last)

    (sh_p, sc_p), (sh_s, sc_s) = mods(kv_mod_w, kv_mod_b, 2)
    kvp = dict(norm=kv_norm, w_k=kv_w[:, :q_dim].astype(BF16), w_v=kv_w[:, q_dim:].astype(BF16),
               k_norm=kv_k_norm)
    k_p, v_p, k_p16, v_p16 = _shared_kv(xp, sh_p, sc_p, kvp)
    k_s, v_s, _, _ = _shared_kv(xs, sh_s, sc_s, kvp)

    n_pool = cache_k.shape[0]
    ck = cache_k.reshape(n_pool, PAGE, -1)
    cv = cache_v.reshape(n_pool, PAGE, -1)
    pad_tok = lambda a, n: jnp.pad(a.reshape(bs, t_s, -1), ((0, 0), (0, n - t_s), (0, 0)))

    for i in range(n_b):
        layer = n_a + i
        lam_init = 0.8 - 0.6 * math.exp(-0.3 * layer)
        qp = dict(norm=b_norm[i], w_q=b_w_in[i][:, :q_dim].astype(BF16),
                  w_z=b_w_in[i][:, q_dim:].astype(BF16), q_norm=b_q_norm[i])
        (sh_p, sc_p, gt_p), (sh_s, sc_s, gt_s) = mods(b_mod_w[i], b_mod_b[i], 3)
        w_out = b_w_out[i].astype(BF16)

        q16, gz = _diff_queries(xp, sh_p, sc_p, qp)
        o = _attn_prompt(q16, k_p16, v_p16, b_lam[i], b_subln[i], lam_init, tq=min(256, t_p))
        xp = _diff_out(xp, o, gz, gt_p, w_out)

        q16, gz = _diff_queries(xs, sh_s, sc_s, qp)
        o = _attn_paged(pad_tok(q16.astype(F32), TSLOT), pad_tok(k_s, PAGE), pad_tok(v_s, PAGE), ck, cv, page_table,
                        b_lam[i], b_subln[i], lam_init, t_s)
        xs = _diff_out(xs, o[:, :t_s].reshape(bs * t_s, -1), gz, gt_s, w_out)

    heads_d = q_dim // LANES
    return (xp.reshape(bp, t_p, d), xs.reshape(bs, t_s, d),
            jnp.stack(wkv_p), jnp.stack(shift_p),
            k_p.reshape(bp, t_p, heads_d, LANES), v_p.reshape(bp, t_p, heads_d, LANES),
            jnp.stack(wkv_s), jnp.stack(shift_s),
            k_s.reshape(bs, t_s, heads_d, LANES), v_s.reshape(bs, t_s, heads_d, LANES))
```

```python
import functools
import math

import jax
import jax.numpy as jnp
from jax import lax
from jax.experimental import pallas as pl
from jax.experimental.pallas import tpu as pltpu

F32 = jnp.float32
BF16 = jnp.bfloat16
HIGHEST = lax.Precision.HIGHEST

LANES = 128
RWKV_HEAD = 64
DIFF_HEAD = 64
PAGE = 128
RMS_EPS = 1e-6
LNX_EPS = 64e-5
NEG_INF = -1e30
ATTN_SCALE = DIFF_HEAD ** -0.5
LOG2E = 1.4426950408889634
ATTN_SUB = 256
WKV_CHUNK = 64
WKV_PAIRS = 8
VMEM_LIMIT = 56 * 1024 * 1024

_NT = (((1,), (1,)), ((), ()))
_TN = (((0,), (0,)), ((), ()))


def _dot(a, b):
    return jnp.dot(a.astype(BF16), b.astype(BF16), preferred_element_type=F32)


def _dot_nt(a, b):
    return lax.dot_general(a.astype(BF16), b.astype(BF16), _NT, preferred_element_type=F32)


def _dot_tn(a, b):
    return lax.dot_general(a.astype(BF16), b.astype(BF16), _TN, preferred_element_type=F32)


def _silu(x):
    return x * (1.0 / (1.0 + jnp.exp(-x)))


def _col_tile(n):
    return next(t for t in (512, 384, 256, 128) if n % t == 0)


def _params(sem):
    return pltpu.CompilerParams(dimension_semantics=sem, vmem_limit_bytes=VMEM_LIMIT)


def _half_mask():
    return lax.broadcasted_iota(jnp.int32, (1, LANES), 1) < 64


def _group64_sum(x, m0):
    s0 = jnp.sum(jnp.where(m0, x, 0.0), axis=-1, keepdims=True)
    s1 = jnp.sum(jnp.where(m0, 0.0, x), axis=-1, keepdims=True)
    return jnp.where(m0, s0, s1)


def _mod_body(c_ref, w_ref, b_ref, o_ref):
    o_ref[...] = jnp.dot(_silu(c_ref[...]), w_ref[...], precision=HIGHEST,
                         preferred_element_type=F32) + b_ref[...]


def _mod_vectors(c, w, b):
    m, d = c.shape
    n = w.shape[1]
    tn = _col_tile(n)
    return pl.pallas_call(
        _mod_body,
        grid=(n // tn,),
        in_specs=[pl.BlockSpec((m, d), lambda j: (0, 0)),
                  pl.BlockSpec((d, tn), lambda j: (0, j)),
                  pl.BlockSpec((1, tn), lambda j: (0, j))],
        out_specs=pl.BlockSpec((m, tn), lambda j: (0, j)),
        out_shape=jax.ShapeDtypeStruct((m, n), F32),
        compiler_params=_params(("arbitrary",)),
        name="mod_vectors",
    )(c, w, b.reshape(1, n))


def _modnorm_val(x, g, shift, scale):
    r = lax.rsqrt(jnp.mean(x * x, axis=-1, keepdims=True) + RMS_EPS)
    return (x * r) * g * (1.0 + scale) + shift


def _modnorm_body(x_ref, g_ref, sh_ref, sc_ref, o_ref):
    o_ref[...] = _modnorm_val(x_ref[...], g_ref[...], sh_ref[...], sc_ref[...])


def _modnorm(x, g, shift, scale, tm):
    m, d = x.shape
    rb = shift.shape[0]
    return pl.pallas_call(
        _modnorm_body,
        grid=(m // tm,),
        in_specs=[pl.BlockSpec((tm, d), lambda i: (i, 0)),
                  pl.BlockSpec((1, d), lambda i: (0, 0)),
                  pl.BlockSpec((rb, d), lambda i: (0, 0)),
                  pl.BlockSpec((rb, d), lambda i: (0, 0))],
        out_specs=pl.BlockSpec((tm, d), lambda i: (i, 0)),
        out_shape=jax.ShapeDtypeStruct((m, d), F32),
        compiler_params=_params(("arbitrary",)),
        name="modnorm",
    )(x, g.reshape(1, d), shift, scale)


def _fused_linear(name, prologue, epilogue, prow, pbc, w, erow, ebc, out_dtypes, *, tm, tn, groups=1):
    prow = [a if isinstance(a, tuple) else (a, 0) for a in prow]
    prow_blk = [cb for _, cb in prow]
    prow = [a for a, _ in prow]
    m = prow[0].shape[0]
    k, n = w.shape
    nj = n // tn
    njg = nj // groups
    n_prow, n_pbc, n_erow, n_ebc, n_out = len(prow), len(pbc), len(erow), len(ebc), len(out_dtypes)

    def body(*refs):
        it = iter(refs)
        prow_r = [next(it) for _ in range(n_prow)]
        pbc_r = [next(it) for _ in range(n_pbc)]
        w_r = next(it)
        erow_r = [next(it) for _ in range(n_erow)]
        ebc_r = [next(it) for _ in range(n_ebc)]
        out_r = [next(it) for _ in range(n_out)]
        xs = next(it)
        j = pl.program_id(1)

        @pl.when(j % njg == 0)
        def _():
            xs[...] = prologue(prow_r, pbc_r, j // njg).astype(BF16)

        acc = jnp.dot(xs[...], w_r[...], preferred_element_type=F32)
        res = epilogue(acc, erow_r, ebc_r)
        for o, r in zip(out_r, res):
            o[...] = r.astype(o.dtype)

    in_specs = ([pl.BlockSpec((tm, k), lambda i, j, cb=cb: (i, cb)) for cb in prow_blk]
                + [pl.BlockSpec(a.shape, lambda i, j: (0, 0)) for a in pbc]
                + [pl.BlockSpec((k, tn), lambda i, j: (0, j))]
                + [pl.BlockSpec((tm, tn), lambda i, j: (i, j)) for _ in erow]
                + [pl.BlockSpec((a.shape[0], tn), lambda i, j: (0, j)) for a in ebc])
    outs = pl.pallas_call(
        body,
        grid=(m // tm, nj),
        in_specs=in_specs,
        out_specs=[pl.BlockSpec((tm, tn), lambda i, j: (i, j)) for _ in out_dtypes],
        out_shape=[jax.ShapeDtypeStruct((m, n), dt) for dt in out_dtypes],
        scratch_shapes=[pltpu.VMEM((tm, k), BF16)],
        compiler_params=_params(("arbitrary", "arbitrary")),
        name=name,
    )(*prow, *pbc, w, *erow, *ebc)
    return outs


def _plain_epilogue(acc, erow_r, ebc_r):
    return (acc,)


def _residual_epilogue(acc, erow_r, ebc_r):
    return (erow_r[0][...] + ebc_r[0][...] * acc,)


def _group_rms_epilogue(with_bf16):
    def epilogue(acc, erow_r, ebc_r):
        m0 = _half_mask()
        gain = ebc_r[0][...]
        cols = []
        for s in range(acc.shape[1] // LANES):
            x = acc[:, s * LANES:(s + 1) * LANES]
            ms = _group64_sum(x * x, m0) * (1.0 / 64.0)
            cols.append(x * lax.rsqrt(ms + RMS_EPS) * gain[:, s * LANES:(s + 1) * LANES])
        y = jnp.concatenate(cols, axis=1) if len(cols) > 1 else cols[0]
        return (y, y) if with_bf16 else (y,)
    return epilogue


def _modnorm_prologue(prow_r, pbc_r, g):
    return _modnorm_val(prow_r[0][...], pbc_r[0][...], pbc_r[1][...], pbc_r[2][...])


def _mix_prologue(prow_r, pbc_r, g):
    h = prow_r[0][...]
    mu = pbc_r[0][pl.ds(g, 1), :]
    return h + (prow_r[1][...] - h) * mu


def _gated_prologue(prow_r, pbc_r, g):
    return prow_r[0][...] * _silu(prow_r[1][...])


def _lora_body(h_ref, hp_ref, mu_ref, w0_ref, w1_ref, w2_ref, a0_ref, a1_ref, a2_ref, lw_ref, cum_ref, a_ref,
               *, seg):
    h = h_ref[...]
    dx = hp_ref[...] - h
    xw = h + dx * mu_ref[4:5, :]
    xa = h + dx * mu_ref[5:6, :]
    wl = w0_ref[...] + _dot(jnp.tanh(_dot(xw, w1_ref[...])), w2_ref[...])
    z = -wl
    softplus = jnp.maximum(z, 0.0) + jnp.log(1.0 + jnp.exp(-jnp.abs(z)))
    lw = -jnp.exp(-softplus - 0.5)
    lw_ref[...] = lw
    tm = lw.shape[0]
    row = lax.broadcasted_iota(jnp.int32, (tm, tm), 0)
    col = lax.broadcasted_iota(jnp.int32, (tm, tm), 1)
    seg_shift = int(math.log2(seg))
    tril = ((row >= col) & ((row >> seg_shift) == (col >> seg_shift))).astype(F32)
    cum_ref[...] = jnp.dot(tril, lw, precision=HIGHEST, preferred_element_type=F32)
    al = a0_ref[...] + _dot(_dot(xa, a1_ref[...]), a2_ref[...])
    a_ref[...] = 1.0 / (1.0 + jnp.exp(-al))


def _lora(h, hprev, mu, w0, w1, w2, a0, a1, a2, tm, seg):
    m, d = h.shape
    assert tm % seg == 0 and seg & (seg - 1) == 0
    row = pl.BlockSpec((tm, d), lambda i: (i, 0))
    full = lambda a: pl.BlockSpec(a.shape, lambda i: (0, 0))
    args = (h, hprev, mu, w0.reshape(1, d), w1, w2, a0.reshape(1, d), a1, a2)
    return pl.pallas_call(
        functools.partial(_lora_body, seg=seg),
        grid=(m // tm,),
        in_specs=[row, row] + [full(a) for a in args[2:]],
        out_specs=[row, row, row],
        out_shape=[jax.ShapeDtypeStruct((m, d), F32)] * 3,
        compiler_params=_params(("arbitrary",)),
        name="rwkv_lora",
    )(*args)


def _wkv_body(r_ref, k_ref, v_ref, cum_ref, lw_ref, a_ref, kk_ref, ka_ref, rk_ref, lnw_ref, lnb_ref, s0_ref,
              o_ref, sf_ref, z_ref, *, chunk, n_chunks, pairs):
    c2 = 2 * chunk
    t_blk = pl.program_id(2)
    m0 = _half_mask()

    @pl.when(t_blk == 0)
    def _():
        zeros = jnp.zeros((RWKV_HEAD, RWKV_HEAD), F32)
        for pi in range(pairs):
            top = jnp.concatenate([s0_ref[0, 2 * pi], zeros], axis=1)
            bot = jnp.concatenate([zeros, s0_ref[0, 2 * pi + 1]], axis=1)
            z_ref[pi] = jnp.concatenate([top, bot], axis=0)

    row = lax.broadcasted_iota(jnp.int32, (c2, c2), 0)
    col = lax.broadcasted_iota(jnp.int32, (c2, c2), 1)
    strict = row > col
    eye = (row == col).astype(F32)
    incl2 = (lax.broadcasted_iota(jnp.int32, (c2, 2 * c2), 0)
             >= (lax.broadcasted_iota(jnp.int32, (c2, 2 * c2), 1) & (c2 - 1)))
    n_double = int(math.log2(chunk)) - 1

    def stack(x):
        return jnp.concatenate([jnp.where(m0, x, 0.0), jnp.where(m0, 0.0, x)], axis=0)

    prs = range(pairs)
    lanes = [slice(p * LANES, (p + 1) * LANES) for p in prs]

    def step(c, carry):
        sl = pl.ds(pl.multiple_of(c * chunk, chunk), chunk)
        r = [r_ref[sl, l] for l in lanes]
        kr = [k_ref[sl, l] for l in lanes]
        v = [v_ref[sl, l] for l in lanes]
        cum = [cum_ref[sl, l] for l in lanes]
        lw = [lw_ref[sl, l] for l in lanes]
        asg = [a_ref[sl, l] for l in lanes]

        kk = [kr[p] * kk_ref[:, lanes[p]] for p in prs]
        kk = [x / jnp.maximum(jnp.sqrt(_group64_sum(x * x, m0)), 1e-12) for x in kk]
        k = [kr[p] * (1.0 + (asg[p] - 1.0) * ka_ref[:, lanes[p]]) for p in prs]

        cum_end = [x[chunk - 1:chunk, :] for x in cum]
        e_out = [jnp.exp(cum_end[p] - cum[p]) for p in prs]
        a_hat = [-kk[p] * jnp.exp(cum[p] - lw[p] - cum_end[p]) for p in prs]
        r_hat = [r[p] * jnp.exp(cum[p] - cum_end[p]) for p in prs]
        b_hat = [kk[p] * asg[p] * e_out[p] for p in prs]
        k_hat = [k[p] * e_out[p] for p in prs]

        ar = [jnp.concatenate([stack(a_hat[p]), stack(r_hat[p])], axis=0) for p in prs]
        bk = [jnp.concatenate([stack(b_hat[p]), stack(k_hat[p])], axis=0) for p in prs]
        v_st = [stack(x) for x in v]

        mm = [_dot_nt(ar[p], bk[p]) for p in prs]
        l_ab = [jnp.where(strict, x[:c2, :c2], 0.0) for x in mm]
        l_ak = [jnp.where(strict, x[:c2, c2:], 0.0) for x in mm]
        l_r = [jnp.where(incl2, x[c2:, :], 0.0) for x in mm]

        t_inv = [eye + x for x in l_ab]
        l_pow = l_ab
        for _ in range(n_double):
            l_pow = [_dot(x, x) for x in l_pow]
            t_inv = [t + _dot(x, t) for x, t in zip(l_pow, t_inv)]

        lv = [_dot(l_ak[p], v_st[p]) for p in prs]
        zd = [z_ref[p] * jnp.exp(cum_end[p]) for p in prs]
        pz = [_dot_nt(ar[p], zd[p]) for p in prs]
        u = [_dot(t_inv[p], pz[p][:c2] + lv[p]) for p in prs]
        uv = [jnp.concatenate([u[p], v_st[p]], axis=0) for p in prs]
        y_st = [pz[p][c2:] + _dot(l_r[p], uv[p]) for p in prs]
        for p in prs:
            z_ref[p] = zd[p] + _dot_tn(uv[p], bk[p])

        for p in prs:
            y = y_st[p][:chunk] + y_st[p][chunk:]
            mean = _group64_sum(y, m0) * (1.0 / 64.0)
            yc = y - mean
            var = _group64_sum(yc * yc, m0) * (1.0 / 64.0)
            bonus = _group64_sum(r[p] * k[p] * rk_ref[:, lanes[p]], m0) * v[p]
            o_ref[sl, lanes[p]] = (yc * lax.rsqrt(var + LNX_EPS) * lnw_ref[:, lanes[p]]
                                   + lnb_ref[:, lanes[p]] + bonus)
        return carry

    lax.fori_loop(0, n_chunks, step, 0)

    @pl.when(t_blk == pl.num_programs(2) - 1)
    def _():
        for pi in range(pairs):
            z = z_ref[pi]
            sf_ref[0, 2 * pi] = z[:RWKV_HEAD, :RWKV_HEAD]
            sf_ref[0, 2 * pi + 1] = z[RWKV_HEAD:, RWKV_HEAD:]


def _wkv(rkvg, cum, lw, asig, k_k, k_a, r_k, lnx_w, lnx_b, s0, *, batch, t_len, t_blk):
    m, d = lw.shape
    pairs = next(p for p in (WKV_PAIRS, 2, 1) if (d // LANES) % p == 0)
    width = pairs * LANES
    hp = d // width
    nt = t_len // t_blk
    heads = d // RWKV_HEAD
    vec = lambda a: a.reshape(1, d)
    row = lambda off: pl.BlockSpec((t_blk, width), lambda b, p, t, off=off: (b * nt + t, off + p))
    par = pl.BlockSpec((1, width), lambda b, p, t: (0, p))
    st = pl.BlockSpec((1, 2 * pairs, RWKV_HEAD, RWKV_HEAD), lambda b, p, t: (b, p, 0, 0))
    body = functools.partial(_wkv_body, chunk=WKV_CHUNK, n_chunks=t_blk // WKV_CHUNK, pairs=pairs)
    return pl.pallas_call(
        body,
        grid=(batch, hp, nt),
        in_specs=[row(0), row(hp), row(2 * hp), row(0), row(0), row(0), par, par, par, par, par, st],
        out_specs=[row(0), st],
        out_shape=[jax.ShapeDtypeStruct((m, d), F32),
                   jax.ShapeDtypeStruct((batch, heads, RWKV_HEAD, RWKV_HEAD), F32)],
        scratch_shapes=[pltpu.VMEM((pairs, LANES, LANES), F32)],
        compiler_params=_params(("arbitrary", "arbitrary", "arbitrary")),
        name="wkv7_chunked",
    )(rkvg, rkvg, rkvg, cum, lw, asig, vec(k_k), vec(k_a), vec(r_k), vec(lnx_w), vec(lnx_b), s0)


def _lambda_val(lam_ref, lam_init):
    lq = lam_ref[...]
    s1 = jnp.sum(lq[0:1] * lq[1:2], axis=-1, keepdims=True)
    s2 = jnp.sum(lq[2:3] * lq[3:4], axis=-1, keepdims=True)
    return jnp.exp(s1) - jnp.exp(s2) + lam_init


def _sub_ln(o, g, lam_init):
    return o * lax.rsqrt(jnp.mean(o * o, axis=-1, keepdims=True) + RMS_EPS) * g * (1.0 - lam_init)


def _head_slope(h, heads):
    return jnp.exp2((-8.0 / heads) * (h + 1).astype(F32))


def _online_softmax_step(s, v_aug, m_ref, l_ref, acc_ref):
    tiles = [s[:, c * LANES:(c + 1) * LANES] for c in range(s.shape[1] // LANES)]
    mt = functools.reduce(jnp.maximum, tiles)
    m_prev = m_ref[...]
    m_next = jnp.maximum(m_prev, jnp.max(mt, axis=-1, keepdims=True))
    p = [jnp.exp2(t - m_next).astype(BF16) for t in tiles]
    p = jnp.concatenate(p, axis=1) if len(p) > 1 else p[0]
    alpha = jnp.exp2(m_prev - m_next)
    pv = jnp.dot(p, v_aug, preferred_element_type=F32)
    acc_ref[...] = alpha * acc_ref[...] + pv[:, :LANES]
    l_ref[...] = alpha * l_ref[...] + pv[:, LANES:]
    m_ref[...] = m_next


def _attn_body(q_ref, k_ref, v_ref, lam_ref, g_ref, o_ref, q2_ref, va_ref, m_ref, l_ref, acc_ref, *,
               tq, wide, heads, lam_init):
    h = pl.program_id(0)
    i = pl.program_id(1)
    m0 = _half_mask()
    slope = _head_slope(jnp.full((1, 1), h, jnp.int32), heads) * LOG2E

    @pl.when(i == 0)
    def _():
        va_ref[:, :LANES] = v_ref[...]
        va_ref[:, LANES:] = jnp.ones(v_ref.shape, BF16)

    q = q_ref[...].astype(F32) * (ATTN_SCALE * LOG2E)
    q2_ref[...] = jnp.concatenate([jnp.where(m0, q, 0.0), jnp.where(m0, 0.0, q)], axis=0).astype(BF16)
    m_ref[...] = jnp.full(m_ref.shape, NEG_INF, F32)
    l_ref[...] = jnp.zeros(l_ref.shape, F32)
    acc_ref[...] = jnp.zeros(acc_ref.shape, F32)

    q_start = i * tq

    def block(k_start, width, masked):
        n_sub = max(1, width // ATTN_SUB)
        sub = width // n_sub
        sls = [pl.ds(pl.multiple_of(k_start + c * sub, sub), sub) for c in range(n_sub)]
        col = lax.broadcasted_iota(jnp.int32, (1, sub), 1)
        ss = [lax.dot_general(q2_ref[...], k_ref[sl, :], _NT, preferred_element_type=F32) for sl in sls]
        for c in range(n_sub):
            s = ss[c] + slope * (k_start + c * sub - q_start + col).astype(F32)
            if masked:
                rr = lax.broadcasted_iota(jnp.int32, (2 * tq, sub), 0) & (tq - 1)
                cc = lax.broadcasted_iota(jnp.int32, (2 * tq, sub), 1)
                s = jnp.where(cc <= rr, s, NEG_INF)
            _online_softmax_step(s, va_ref[sls[c], :], m_ref, l_ref, acc_ref)

    n_wide = q_start // wide
    n_narrow = (q_start - n_wide * wide) // tq

    def wide_block(j, carry):
        block(j * wide, wide, False)
        return carry

    def narrow_block(j, carry):
        block(n_wide * wide + j * tq, tq, False)
        return carry

    lax.fori_loop(0, n_wide, wide_block, 0)
    lax.fori_loop(0, n_narrow, narrow_block, 0)
    block(q_start, tq, True)

    lam = _lambda_val(lam_ref, lam_init)
    inv_l = 1.0 / l_ref[...]
    acc = acc_ref[...] * inv_l
    o = acc[:tq] - lam * acc[tq:]
    o_ref[...] = _sub_ln(o, g_ref[...], lam_init)


def _attn_prompt(q, k, v, lam, subln, lam_init, tq):
    t, hd = q.shape
    heads = hd // LANES
    wide = min(4 * tq, t)
    body = functools.partial(_attn_body, tq=tq, wide=wide, heads=heads, lam_init=lam_init)
    return pl.pallas_call(
        body,
        grid=(heads, t // tq),
        in_specs=[pl.BlockSpec((tq, LANES), lambda h, i: (i, h)),
                  pl.BlockSpec((t, LANES), lambda h, i: (0, h)),
                  pl.BlockSpec((t, LANES), lambda h, i: (0, h)),
                  pl.BlockSpec(lam.shape, lambda h, i: (0, 0)),
                  pl.BlockSpec((1, LANES), lambda h, i: (0, 0))],
        out_specs=pl.BlockSpec((tq, LANES), lambda h, i: (i, h)),
        out_shape=jax.ShapeDtypeStruct((t, hd), F32),
        scratch_shapes=[pltpu.VMEM((2 * tq, LANES), BF16),
                        pltpu.VMEM((t, 2 * LANES), BF16),
                        pltpu.VMEM((2 * tq, LANES), F32),
                        pltpu.VMEM((2 * tq, LANES), F32),
                        pltpu.VMEM((2 * tq, LANES), F32)],
        compiler_params=_params(("arbitrary", "arbitrary")),
        name="diff_attn_prompt",
    )(q, k, v, lam, subln.reshape(1, LANES))


TSLOT = 8
PAGES_PER_STEP = 4
PAGE_GROUPS = 4


def _paged_body(pt_ref, q_ref, *refs, heads, n_pages, t_new, lam_init):
    ck_refs = refs[:PAGES_PER_STEP]
    cv_refs = refs[PAGES_PER_STEP:2 * PAGES_PER_STEP]
    kn_ref, vn_ref, lam_ref, g_ref, o_ref, q2_ref, b0_ref, m_ref, l_ref, acc_ref = refs[2 * PAGES_PER_STEP:]
    j = pl.program_id(1)
    n_steps = n_pages // PAGES_PER_STEP
    ng = min(PAGE_GROUPS, heads)
    hpg = heads // ng
    hrows = 2 * TSLOT
    grows = hpg * hrows
    rows = heads * hrows
    m0 = _half_mask()
    sh = lambda n: int(math.log2(n))

    ridx = lax.broadcasted_iota(jnp.int32, (rows, 1), 0)
    row_hh = (ridx >> sh(hrows)) & (hpg - 1)
    slope = _head_slope((ridx >> sh(grows)) + ng * row_hh, heads)
    tpos = ridx & (TSLOT - 1)

    def key_layout(width):
        col = lax.broadcasted_iota(jnp.int32, (1, width), 1)
        return col >> sh(hpg), col & (hpg - 1)

    @pl.when((pl.program_id(0) == 0) & (j == 0))
    def _():
        width = PAGES_PER_STEP * PAGE * hpg
        tok, col_hh = key_layout(width)
        dist = (PAGES_PER_STEP * PAGE - tok + tpos).astype(F32)
        b0_ref[...] = jnp.where(col_hh == row_hh, -slope * dist, NEG_INF)

    @pl.when(j == 0)
    def _():
        q = q_ref[0].astype(F32) * ATTN_SCALE
        for h in range(heads):
            qh = q[:, h * LANES:(h + 1) * LANES]
            base = (h % ng) * grows + (h // ng) * hrows
            q2_ref[base:base + hrows, :] = jnp.concatenate(
                [jnp.where(m0, qh, 0.0), jnp.where(m0, 0.0, qh)], axis=0).astype(BF16)
        m_ref[...] = jnp.full(m_ref.shape, NEG_INF, F32)
        l_ref[...] = jnp.zeros(l_ref.shape, F32)
        acc_ref[...] = jnp.zeros(acc_ref.shape, F32)

    def attend(k_tiles, v_tiles, bias):
        def gather(tiles, g):
            parts = [r[0, pl.ds(t * LANES * ng + g, LANES, stride=ng), :] for r, t in tiles]
            x = jnp.concatenate(parts, axis=0) if len(parts) > 1 else parts[0]
            return x.astype(BF16)

        s = jnp.concatenate(
            [lax.dot_general(q2_ref[g * grows:(g + 1) * grows, :], gather(k_tiles, g), _NT,
                             preferred_element_type=F32) for g in range(ng)], axis=0) + bias
        tiles = [s[:, c * LANES:(c + 1) * LANES] for c in range(s.shape[1] // LANES)]
        m_prev = m_ref[...]
        m_next = jnp.maximum(m_prev, jnp.max(functools.reduce(jnp.maximum, tiles), axis=-1, keepdims=True))
        p = [jnp.exp(t - m_next) for t in tiles]
        alpha = jnp.exp(m_prev - m_next)
        l_ref[...] = alpha * l_ref[...] + jnp.sum(functools.reduce(jnp.add, p), axis=-1, keepdims=True)
        p = [t.astype(BF16) for t in p]
        p = jnp.concatenate(p, axis=1) if len(p) > 1 else p[0]
        pv = jnp.concatenate(
            [jnp.dot(p[g * grows:(g + 1) * grows, :], gather(v_tiles, g), preferred_element_type=F32)
             for g in range(ng)], axis=0)
        acc_ref[...] = alpha * acc_ref[...] + pv
        m_ref[...] = m_next

    @pl.when(j < n_steps)
    def _():
        gap = ((n_steps - 1 - j) * (PAGES_PER_STEP * PAGE)).astype(F32)
        attend([(r, t) for r in ck_refs for t in range(hpg)],
               [(r, t) for r in cv_refs for t in range(hpg)],
               b0_ref[...] - slope * gap)

    @pl.when(j == n_steps)
    def _():
        tok, col_hh = key_layout(LANES)
        valid = (col_hh == row_hh) & (tok <= tpos) & (tok < t_new)
        bias = jnp.where(valid, -slope * (tpos - tok).astype(F32), NEG_INF)
        attend([(kn_ref, 0)], [(vn_ref, 0)], bias)
        lam = _lambda_val(lam_ref, lam_init)
        acc = acc_ref[...] * (1.0 / l_ref[...])
        for h in range(heads):
            base = (h % ng) * grows + (h // ng) * hrows
            o = acc[base:base + TSLOT] - lam * acc[base + TSLOT:base + 2 * TSLOT]
            o_ref[0, :, h * LANES:(h + 1) * LANES] = _sub_ln(o, g_ref[...], lam_init)


def _attn_paged(q, k_new, v_new, cache_k, cache_v, page_table, lam, subln, lam_init, t_new):
    bsz, _, hd = q.shape
    heads = hd // LANES
    n_pages = page_table.shape[1]
    assert n_pages % PAGES_PER_STEP == 0
    n_steps = n_pages // PAGES_PER_STEP
    rows = heads * 2 * TSLOT
    ng = min(PAGE_GROUPS, heads)
    hpg = heads // ng
    assert heads == ng * hpg and hpg & (hpg - 1) == 0 and t_new <= LANES // hpg and t_new <= TSLOT
    body = functools.partial(_paged_body, heads=heads, n_pages=n_pages, t_new=t_new, lam_init=lam_init)

    def page(a):
        idx = lambda b, j, pt: (pt[b, jnp.minimum(j, n_steps - 1) * PAGES_PER_STEP + a], 0, 0)
        return pl.BlockSpec((1, PAGE * heads, LANES), idx)

    per_b = lambda b, j, pt: (b, 0, 0)
    grid_spec = pltpu.PrefetchScalarGridSpec(
        num_scalar_prefetch=1,
        grid=(bsz, n_steps + 1),
        in_specs=[pl.BlockSpec((1, TSLOT, hd), per_b),
                  *[page(a) for a in range(PAGES_PER_STEP)], *[page(a) for a in range(PAGES_PER_STEP)],
                  pl.BlockSpec((1, LANES * ng, LANES), per_b),
                  pl.BlockSpec((1, LANES * ng, LANES), per_b),
                  pl.BlockSpec(lam.shape, lambda b, j, pt: (0, 0)),
                  pl.BlockSpec((1, LANES), lambda b, j, pt: (0, 0))],
        out_specs=pl.BlockSpec((1, TSLOT, hd), per_b),
        scratch_shapes=[pltpu.VMEM((rows, LANES), BF16),
                        pltpu.VMEM((rows, PAGES_PER_STEP * PAGE * hpg), F32),
                        pltpu.VMEM((rows, LANES), F32),
                        pltpu.VMEM((rows, LANES), F32),
                        pltpu.VMEM((rows, LANES), F32)])
    return pl.pallas_call(
        body,
        grid_spec=grid_spec,
        out_shape=jax.ShapeDtypeStruct((bsz, TSLOT, hd), F32),
        compiler_params=_params(("arbitrary", "arbitrary")),
        name="diff_attn_paged",
    )(page_table, q, *([cache_k] * PAGES_PER_STEP), *([cache_v] * PAGES_PER_STEP), k_new, v_new, lam,
      subln.reshape(1, LANES))


def _row_tile(m):
    return min(512, m)


def _rwkv_layer(x, shift, scale, gate, h_prev_row, s0, p, *, batch, t_len):
    m, d = x.shape
    tm = _row_tile(m)
    tn = _col_tile(d)
    h = _modnorm(x, p["norm"], shift, scale, tm)
    h3 = h.reshape(batch, t_len, d)
    hprev = jnp.concatenate([h_prev_row[:, None, :], h3[:, :-1]], axis=1).reshape(m, d)

    rkvg, = _fused_linear("rwkv_in_proj", _mix_prologue, _plain_epilogue, [h, hprev], [p["mu"]],
                          p["w_in"], [], [], [F32], tm=tm, tn=tn, groups=4)
    seg = WKV_CHUNK if t_len >= WKV_CHUNK else t_len
    assert t_len % seg == 0
    lw, cum, asig = _lora(h, hprev, p["mu"], p["w0"], p["w1"], p["w2"], p["a0"], p["a1"], p["a2"],
                          min(256, m), seg)

    t_pad = -(-t_len // WKV_CHUNK) * WKV_CHUNK
    if t_pad != t_len:
        def padr(a, mode="constant"):
            return jnp.pad(a.reshape(batch, t_len, -1), ((0, 0), (0, t_pad - t_len), (0, 0)),
                           mode=mode).reshape(batch * t_pad, -1)
        rkvg_w, cum_w, lw_w, asig_w = padr(rkvg), padr(cum, "edge"), padr(lw), padr(asig)
    else:
        rkvg_w, cum_w, lw_w, asig_w = rkvg, cum, lw, asig
    t_blk = min(512, t_pad)
    o_pre, s_fin = _wkv(rkvg_w, cum_w, lw_w, asig_w, p["k_k"], p["k_a"], p["r_k"], p["lnx_w"], p["lnx_b"],
                        s0, batch=batch, t_len=t_pad, t_blk=t_blk)
    if t_pad != t_len:
        o_pre = o_pre.reshape(batch, t_pad, d)[:, :t_len].reshape(m, d)

    x_new, = _fused_linear("rwkv_out_proj", _gated_prologue, _residual_epilogue, [o_pre, (rkvg, 3)], [],
                           p["w_out"], [x], [gate], [F32], tm=tm, tn=tn)
    return x_new, s_fin, h3[:, -1]


def _tile_gain(gain, n):
    return jnp.tile(gain.reshape(1, LANES), (1, n // LANES))


def _shared_kv(x, shift, scale, p):
    m, d = x.shape
    tm = _row_tile(m)
    qd = p["w_k"].shape[1]
    pbc = [p["norm"].reshape(1, d), shift, scale]
    k, k16 = _fused_linear("kv_k_proj", _modnorm_prologue, _group_rms_epilogue(True), [x], pbc, p["w_k"],
                           [], [_tile_gain(p["k_norm"], qd)], [F32, BF16], tm=tm, tn=_col_tile(qd))
    v, v16 = _fused_linear("kv_v_proj", _modnorm_prologue, lambda acc, e, b: (acc, acc), [x], pbc, p["w_v"],
                           [], [], [F32, BF16], tm=tm, tn=_col_tile(p["w_v"].shape[1]))
    return k, v, k16, v16


def _diff_queries(x, shift, scale, p):
    m, d = x.shape
    tm = _row_tile(m)
    qd = p["w_q"].shape[1]
    pbc = [p["norm"].reshape(1, d), shift, scale]
    q16, = _fused_linear("diff_q_proj", _modnorm_prologue, _group_rms_epilogue(False), [x], pbc, p["w_q"],
                         [], [_tile_gain(p["q_norm"], qd)], [BF16], tm=tm, tn=_col_tile(qd))
    gz, = _fused_linear("diff_gate_proj", _modnorm_prologue, _plain_epilogue, [x], pbc, p["w_z"],
                        [], [], [F32], tm=tm, tn=_col_tile(p["w_z"].shape[1]))
    return q16, gz


def _diff_out(x, o, gz, gate, w_out):
    m, d = x.shape
    y, = _fused_linear("diff_out_proj", _gated_prologue, _residual_epilogue, [o, gz], [], w_out,
                       [x], [gate], [F32], tm=_row_tile(m), tn=_col_tile(d))
    return y


def kernel(x_prompt, x_sample, c_prompt, c_sample, state_wkv, state_shift, cache_k, cache_v, page_table,
           a_mod_w, a_mod_b, a_norm, a_mu, a_w_in, a_w0, a_w1, a_w2, a_a0, a_a1, a_a2,
           a_k_k, a_k_a, a_r_k, a_lnx_w, a_lnx_b, a_w_out,
           kv_mod_w, kv_mod_b, kv_norm, kv_w, kv_k_norm,
           b_mod_w, b_mod_b, b_norm, b_w_in, b_q_norm, b_lam, b_subln, b_w_out):
    bp, t_p, d = x_prompt.shape
    bs, t_s, _ = x_sample.shape
    n_a = a_mod_w.shape[0]
    n_b = b_mod_w.shape[0]
    depth = n_a + n_b
    heads_r = d // RWKV_HEAD
    q_dim = b_w_in.shape[2] - d
    lora = a_w1.shape[2]
    lora_pad = -(-lora // LANES) * LANES

    xp = x_prompt.reshape(bp * t_p, d)
    xs = x_sample.reshape(bs * t_s, d)

    n_c = bp + bs
    c_all = jnp.pad(jnp.concatenate([c_prompt, c_sample], axis=0), ((0, -n_c % 8), (0, 0)))

    def mods(w, b, n):
        mv = _mod_vectors(c_all, w, b)
        parts = jnp.split(mv, n, axis=-1)
        assert bp == 1
        prompt = [q[0:1] for q in parts]
        sample = [jnp.repeat(q[bp:n_c], t_s, axis=0) for q in parts]
        return prompt, sample

    wkv_p, wkv_s, shift_p, shift_s = [], [], [], []
    for layer in range(n_a):
        prm = dict(
            norm=a_norm[layer], mu=a_mu[layer],
            w_in=a_w_in[layer].reshape(d, 4 * d).astype(BF16),
            w0=a_w0[layer], a0=a_a0[layer],
            w1=jnp.pad(a_w1[layer], ((0, 0), (0, lora_pad - lora))).astype(BF16),
            w2=jnp.pad(a_w2[layer], ((0, lora_pad - lora), (0, 0))).astype(BF16),
            a1=jnp.pad(a_a1[layer], ((0, 0), (0, lora_pad - lora))).astype(BF16),
            a2=jnp.pad(a_a2[layer], ((0, lora_pad - lora), (0, 0))).astype(BF16),
            k_k=a_k_k[layer], k_a=a_k_a[layer], r_k=a_r_k[layer],
            lnx_w=a_lnx_w[layer], lnx_b=a_lnx_b[layer],
            w_out=a_w_out[layer].astype(BF16))
        (sh_p, sc_p, gt_p), (sh_s, sc_s, gt_s) = mods(a_mod_w[layer], a_mod_b[layer], 3)
        xp, s_fin, last = _rwkv_layer(
            xp, sh_p, sc_p, gt_p, jnp.zeros((bp, d), F32),
            jnp.zeros((bp, heads_r, RWKV_HEAD, RWKV_HEAD), F32), prm, batch=bp, t_len=t_p)
        wkv_p.append(s_fin)
        shift_p.append(last)
        xs, s_fin, last = _rwkv_layer(
            xs, sh_s, sc_s, gt_s, state_shift[layer], state_wkv[layer], prm, batch=bs, t_len=t_s)
        wkv_s.append(s_fin)
        shift_s.append(last)

    (sh_p, sc_p), (sh_s, sc_s) = mods(kv_mod_w, kv_mod_b, 2)
    kvp = dict(norm=kv_norm, w_k=kv_w[:, :q_dim].astype(BF16), w_v=kv_w[:, q_dim:].astype(BF16),
               k_norm=kv_k_norm)
    k_p, v_p, k_p16, v_p16 = _shared_kv(xp, sh_p, sc_p, kvp)
    k_s, v_s, _, _ = _shared_kv(xs, sh_s, sc_s, kvp)

    n_pool = cache_k.shape[0]
    pad_tok = lambda a, n: jnp.pad(a.reshape(bs, t_s, -1), ((0, 0), (0, n - t_s), (0, 0)))
    heads_q = q_dim // LANES
    new_tok = LANES * min(PAGE_GROUPS, heads_q) // heads_q
    as_page = lambda a: pad_tok(a, new_tok).reshape(bs, new_tok * heads_q, LANES)

    for i in range(n_b):
        layer = n_a + i
        lam_init = 0.8 - 0.6 * math.exp(-0.3 * layer)
        qp = dict(norm=b_norm[i], w_q=b_w_in[i][:, :q_dim].astype(BF16),
                  w_z=b_w_in[i][:, q_dim:].astype(BF16), q_norm=b_q_norm[i])
        (sh_p, sc_p, gt_p), (sh_s, sc_s, gt_s) = mods(b_mod_w[i], b_mod_b[i], 3)
        w_out = b_w_out[i].astype(BF16)

        q16, gz = _diff_queries(xp, sh_p, sc_p, qp)
        o = _attn_prompt(q16, k_p16, v_p16, b_lam[i], b_subln[i], lam_init, tq=min(256, t_p))
        xp = _diff_out(xp, o, gz, gt_p, w_out)

        q16, gz = _diff_queries(xs, sh_s, sc_s, qp)
        o = _attn_paged(pad_tok(q16.astype(F32), TSLOT), as_page(k_s), as_page(v_s),
                        cache_k.reshape(n_pool, -1, LANES), cache_v.reshape(n_pool, -1, LANES), page_table,
                        b_lam[i], b_subln[i], lam_init, t_s)
        xs = _diff_out(xs, o[:, :t_s].reshape(bs * t_s, -1), gz, gt_s, w_out)

    heads_d = q_dim // LANES
    return (xp.reshape(bp, t_p, d), xs.reshape(bs, t_s, d),
            jnp.stack(wkv_p), jnp.stack(shift_p),
            k_p.reshape(bp, t_p, heads_d, LANES), v_p.reshape(bp, t_p, heads_d, LANES),
            jnp.stack(wkv_s), jnp.stack(shift_s),
            k_s.reshape(bs, t_s, heads_d, LANES), v_s.reshape(bs, t_s, heads_d, LANES))
```

```python
import functools
import math

import jax
import jax.numpy as jnp
from jax import lax
from jax.experimental import pallas as pl
from jax.experimental.pallas import tpu as pltpu

F32 = jnp.float32
BF16 = jnp.bfloat16
HIGHEST = lax.Precision.HIGHEST

LANES = 128
RWKV_HEAD = 64
DIFF_HEAD = 64
PAGE = 128
RMS_EPS = 1e-6
LNX_EPS = 64e-5
NEG_INF = -1e30
ATTN_SCALE = DIFF_HEAD ** -0.5
LOG2E = 1.4426950408889634
EPILOGUE_ROWS = 256
WKV_CHUNK = 64
WKV_PAIRS = 8
VMEM_LIMIT = 56 * 1024 * 1024

_NT = (((1,), (1,)), ((), ()))
_TN = (((0,), (0,)), ((), ()))


def _dot(a, b):
    return jnp.dot(a.astype(BF16), b.astype(BF16), preferred_element_type=F32)


def _dot_nt(a, b):
    return lax.dot_general(a.astype(BF16), b.astype(BF16), _NT, preferred_element_type=F32)


def _dot_tn(a, b):
    return lax.dot_general(a.astype(BF16), b.astype(BF16), _TN, preferred_element_type=F32)


def _silu(x):
    return x * (1.0 / (1.0 + jnp.exp(-x)))


def _col_tile(n):
    return next(t for t in (512, 384, 256, 128) if n % t == 0)


def _wide_col_tile(n):
    return next(t for t in (1024, 512, 384, 256, 128) if n % t == 0)


def _params(sem):
    return pltpu.CompilerParams(dimension_semantics=sem, vmem_limit_bytes=VMEM_LIMIT)


def _half_mask():
    return lax.broadcasted_iota(jnp.int32, (1, LANES), 1) < 64


def _group64_sum(x, m0):
    s0 = jnp.sum(jnp.where(m0, x, 0.0), axis=-1, keepdims=True)
    s1 = jnp.sum(jnp.where(m0, 0.0, x), axis=-1, keepdims=True)
    return jnp.where(m0, s0, s1)


def _mod_body(c_ref, w_ref, b_ref, o_ref):
    o_ref[...] = jnp.dot(_silu(c_ref[...]), w_ref[...], precision=HIGHEST,
                         preferred_element_type=F32) + b_ref[...]


def _mod_vectors(c, w, b):
    m, d = c.shape
    n = w.shape[1]
    tn = _col_tile(n)
    return pl.pallas_call(
        _mod_body,
        grid=(n // tn,),
        in_specs=[pl.BlockSpec((m, d), lambda j: (0, 0)),
                  pl.BlockSpec((d, tn), lambda j: (0, j)),
                  pl.BlockSpec((1, tn), lambda j: (0, j))],
        out_specs=pl.BlockSpec((m, tn), lambda j: (0, j)),
        out_shape=jax.ShapeDtypeStruct((m, n), F32),
        compiler_params=_params(("arbitrary",)),
        name="mod_vectors",
    )(c, w, b.reshape(1, n))


def _modnorm_val(x, g, shift, scale):
    r = lax.rsqrt(jnp.mean(x * x, axis=-1, keepdims=True) + RMS_EPS)
    return (x * r) * g * (1.0 + scale) + shift


def _modnorm_body(x_ref, g_ref, sh_ref, sc_ref, o_ref):
    o_ref[...] = _modnorm_val(x_ref[...], g_ref[...], sh_ref[...], sc_ref[...]).astype(o_ref.dtype)


def _modnorm(x, g, shift, scale, tm, out_dtype=F32):
    m, d = x.shape
    rb = shift.shape[0]
    return pl.pallas_call(
        _modnorm_body,
        grid=(m // tm,),
        in_specs=[pl.BlockSpec((tm, d), lambda i: (i, 0)),
                  pl.BlockSpec((1, d), lambda i: (0, 0)),
                  pl.BlockSpec((rb, d), lambda i: (0, 0)),
                  pl.BlockSpec((rb, d), lambda i: (0, 0))],
        out_specs=pl.BlockSpec((tm, d), lambda i: (i, 0)),
        out_shape=jax.ShapeDtypeStruct((m, d), out_dtype),
        compiler_params=_params(("arbitrary",)),
        name="modnorm",
    )(x, g.reshape(1, d), shift, scale)


def _fused_linear(name, prologue, epilogue, prow, pbc, w, erow, ebc, out_dtypes, *, tm, tn, groups=1):
    prow = [a if isinstance(a, tuple) else (a, 0) for a in prow]
    prow_blk = [cb for _, cb in prow]
    prow = [a for a, _ in prow]
    m = prow[0].shape[0]
    k, n = w.shape
    nj = n // tn
    njg = nj // groups
    rc = min(EPILOGUE_ROWS, tm)
    n_prow, n_pbc, n_erow, n_ebc, n_out = len(prow), len(pbc), len(erow), len(ebc), len(out_dtypes)

    def body(*refs):
        it = iter(refs)
        prow_r = [next(it) for _ in range(n_prow)]
        pbc_r = [next(it) for _ in range(n_pbc)]
        w_r = next(it)
        erow_r = [next(it) for _ in range(n_erow)]
        ebc_r = [next(it) for _ in range(n_ebc)]
        out_r = [next(it) for _ in range(n_out)]
        xs = next(it) if prologue is not None else prow_r[0]
        acc_s = next(it) if epilogue is not None else None
        j = pl.program_id(1)

        if prologue is not None:
            @pl.when(j % njg == 0)
            def _():
                xs[...] = prologue(prow_r, pbc_r, j // njg).astype(BF16)

        acc = jnp.dot(xs[...], w_r[...], preferred_element_type=F32)
        if epilogue is None:
            for o in out_r:
                o[...] = acc.astype(o.dtype)
            return
        acc_s[...] = acc

        def chunk(c, carry):
            rs = pl.ds(pl.multiple_of(c * rc, rc), rc)
            res = epilogue(acc_s[rs, :], [r[rs, :] for r in erow_r],
                           [r[rs, :] if r.shape[0] == tm and tm > 1 else r[...] for r in ebc_r])
            for o, r in zip(out_r, res):
                o[rs, :] = r.astype(o.dtype)
            return carry

        lax.fori_loop(0, tm // rc, chunk, 0)

    in_specs = ([pl.BlockSpec((tm, k), lambda i, j, cb=cb: (i, cb)) for cb in prow_blk]
                + [pl.BlockSpec(a.shape, lambda i, j: (0, 0)) for a in pbc]
                + [pl.BlockSpec((k, tn), lambda i, j: (0, j))]
                + [pl.BlockSpec((tm, tn), lambda i, j: (i, j)) for _ in erow]
                + [pl.BlockSpec((a.shape[0], tn), lambda i, j: (0, j)) for a in ebc])
    outs = pl.pallas_call(
        body,
        grid=(m // tm, nj),
        in_specs=in_specs,
        out_specs=[pl.BlockSpec((tm, tn), lambda i, j: (i, j)) for _ in out_dtypes],
        out_shape=[jax.ShapeDtypeStruct((m, n), dt) for dt in out_dtypes],
        scratch_shapes=([pltpu.VMEM((tm, k), BF16)] if prologue is not None else [])
                       + ([pltpu.VMEM((tm, tn), F32)] if epilogue is not None else []),
        compiler_params=_params(("arbitrary", "arbitrary")),
        name=name,
    )(*prow, *pbc, w, *erow, *ebc)
    return outs


def _residual_epilogue(acc, erows, ebcs):
    return (erows[0] + ebcs[0] * acc,)


def _group_rms_epilogue(with_bf16):
    def epilogue(acc, erows, ebcs):
        m0 = _half_mask()
        gain = ebcs[0]
        cols = []
        for s in range(acc.shape[1] // LANES):
            x = acc[:, s * LANES:(s + 1) * LANES]
            ms = _group64_sum(x * x, m0) * (1.0 / 64.0)
            cols.append(x * lax.rsqrt(ms + RMS_EPS) * gain[:, s * LANES:(s + 1) * LANES])
        y = jnp.concatenate(cols, axis=1) if len(cols) > 1 else cols[0]
        return (y, y) if with_bf16 else (y,)
    return epilogue


def _mix_prologue(prow_r, pbc_r, g):
    h = prow_r[0][...]
    mu = pbc_r[0][pl.ds(g, 1), :]
    return h + (prow_r[1][...] - h) * mu


def _lora_body(h_ref, hp_ref, mu_ref, w0_ref, w1_ref, w2_ref, a0_ref, a1_ref, a2_ref, lw_ref, cum_ref, a_ref,
               *, seg):
    h = h_ref[...]
    dx = hp_ref[...] - h
    xw = h + dx * mu_ref[4:5, :]
    xa = h + dx * mu_ref[5:6, :]
    wl = w0_ref[...] + _dot(jnp.tanh(_dot(xw, w1_ref[...])), w2_ref[...])
    z = -wl
    softplus = jnp.maximum(z, 0.0) + jnp.log(1.0 + jnp.exp(-jnp.abs(z)))
    lw = -jnp.exp(-softplus - 0.5)
    lw_ref[...] = lw
    tm = lw.shape[0]
    row = lax.broadcasted_iota(jnp.int32, (tm, tm), 0)
    col = lax.broadcasted_iota(jnp.int32, (tm, tm), 1)
    seg_shift = int(math.log2(seg))
    tril = ((row >= col) & ((row >> seg_shift) == (col >> seg_shift))).astype(F32)
    cum_ref[...] = jnp.dot(tril, lw, precision=HIGHEST, preferred_element_type=F32)
    al = a0_ref[...] + _dot(_dot(xa, a1_ref[...]), a2_ref[...])
    a_ref[...] = 1.0 / (1.0 + jnp.exp(-al))


def _lora(h, hprev, mu, w0, w1, w2, a0, a1, a2, tm, seg):
    m, d = h.shape
    assert tm % seg == 0 and seg & (seg - 1) == 0
    row = pl.BlockSpec((tm, d), lambda i: (i, 0))
    full = lambda a: pl.BlockSpec(a.shape, lambda i: (0, 0))
    args = (h, hprev, mu, w0.reshape(1, d), w1, w2, a0.reshape(1, d), a1, a2)
    return pl.pallas_call(
        functools.partial(_lora_body, seg=seg),
        grid=(m // tm,),
        in_specs=[row, row] + [full(a) for a in args[2:]],
        out_specs=[row, row, row],
        out_shape=[jax.ShapeDtypeStruct((m, d), F32)] * 3,
        compiler_params=_params(("arbitrary",)),
        name="rwkv_lora",
    )(*args)


def _wkv_body(r_ref, k_ref, v_ref, g_ref, cum_ref, lw_ref, a_ref, kk_ref, ka_ref, rk_ref, lnw_ref, lnb_ref, s0_ref,
              o_ref, sf_ref, z_ref, *, chunk, n_chunks, pairs):
    c2 = 2 * chunk
    t_blk = pl.program_id(2)
    m0 = _half_mask()

    @pl.when(t_blk == 0)
    def _():
        zeros = jnp.zeros((RWKV_HEAD, RWKV_HEAD), F32)
        for pi in range(pairs):
            top = jnp.concatenate([s0_ref[0, 2 * pi], zeros], axis=1)
            bot = jnp.concatenate([zeros, s0_ref[0, 2 * pi + 1]], axis=1)
            z_ref[pi] = jnp.concatenate([top, bot], axis=0)

    row = lax.broadcasted_iota(jnp.int32, (c2, c2), 0)
    col = lax.broadcasted_iota(jnp.int32, (c2, c2), 1)
    strict = row > col
    eye = (row == col).astype(F32)
    incl2 = (lax.broadcasted_iota(jnp.int32, (c2, 2 * c2), 0)
             >= (lax.broadcasted_iota(jnp.int32, (c2, 2 * c2), 1) & (c2 - 1)))
    n_double = int(math.log2(chunk)) - 1

    def stack(x):
        return jnp.concatenate([jnp.where(m0, x, 0.0), jnp.where(m0, 0.0, x)], axis=0)

    prs = range(pairs)
    lanes = [slice(p * LANES, (p + 1) * LANES) for p in prs]

    def step(c, carry):
        sl = pl.ds(pl.multiple_of(c * chunk, chunk), chunk)
        r = [r_ref[sl, l] for l in lanes]
        kr = [k_ref[sl, l] for l in lanes]
        v = [v_ref[sl, l] for l in lanes]
        cum = [cum_ref[sl, l] for l in lanes]
        lw = [lw_ref[sl, l] for l in lanes]
        asg = [a_ref[sl, l] for l in lanes]

        kk = [kr[p] * kk_ref[:, lanes[p]] for p in prs]
        kk = [x / jnp.maximum(jnp.sqrt(_group64_sum(x * x, m0)), 1e-12) for x in kk]
        k = [kr[p] * (1.0 + (asg[p] - 1.0) * ka_ref[:, lanes[p]]) for p in prs]

        cum_end = [x[chunk - 1:chunk, :] for x in cum]
        e_out = [jnp.exp(cum_end[p] - cum[p]) for p in prs]
        a_hat = [-kk[p] * jnp.exp(cum[p] - lw[p] - cum_end[p]) for p in prs]
        r_hat = [r[p] * jnp.exp(cum[p] - cum_end[p]) for p in prs]
        b_hat = [kk[p] * asg[p] * e_out[p] for p in prs]
        k_hat = [k[p] * e_out[p] for p in prs]

        ar = [jnp.concatenate([stack(a_hat[p]), stack(r_hat[p])], axis=0) for p in prs]
        bk = [jnp.concatenate([stack(b_hat[p]), stack(k_hat[p])], axis=0) for p in prs]
        v_st = [stack(x) for x in v]

        mm = [_dot_nt(ar[p], bk[p]) for p in prs]
        l_ab = [jnp.where(strict, x[:c2, :c2], 0.0) for x in mm]
        l_ak = [jnp.where(strict, x[:c2, c2:], 0.0) for x in mm]
        l_r = [jnp.where(incl2, x[c2:, :], 0.0) for x in mm]

        t_inv = [eye + x for x in l_ab]
        l_pow = l_ab
        for _ in range(n_double):
            l_pow = [_dot(x, x) for x in l_pow]
            t_inv = [t + _dot(x, t) for x, t in zip(l_pow, t_inv)]

        lv = [_dot(l_ak[p], v_st[p]) for p in prs]
        zd = [z_ref[p] * jnp.exp(cum_end[p]) for p in prs]
        pz = [_dot_nt(ar[p], zd[p]) for p in prs]
        u = [_dot(t_inv[p], pz[p][:c2] + lv[p]) for p in prs]
        uv = [jnp.concatenate([u[p], v_st[p]], axis=0) for p in prs]
        y_st = [pz[p][c2:] + _dot(l_r[p], uv[p]) for p in prs]
        for p in prs:
            z_ref[p] = zd[p] + _dot_tn(uv[p], bk[p])

        for p in prs:
            y = y_st[p][:chunk] + y_st[p][chunk:]
            mean = _group64_sum(y, m0) * (1.0 / 64.0)
            yc = y - mean
            var = _group64_sum(yc * yc, m0) * (1.0 / 64.0)
            bonus = _group64_sum(r[p] * k[p] * rk_ref[:, lanes[p]], m0) * v[p]
            o = yc * lax.rsqrt(var + LNX_EPS) * lnw_ref[:, lanes[p]] + lnb_ref[:, lanes[p]] + bonus
            o_ref[sl, lanes[p]] = (o * _silu(g_ref[sl, lanes[p]])).astype(o_ref.dtype)
        return carry

    lax.fori_loop(0, n_chunks, step, 0)

    @pl.when(t_blk == pl.num_programs(2) - 1)
    def _():
        for pi in range(pairs):
            z = z_ref[pi]
            sf_ref[0, 2 * pi] = z[:RWKV_HEAD, :RWKV_HEAD]
            sf_ref[0, 2 * pi + 1] = z[RWKV_HEAD:, RWKV_HEAD:]


def _wkv(rkvg, cum, lw, asig, k_k, k_a, r_k, lnx_w, lnx_b, s0, *, batch, t_len, t_blk):
    m, d = lw.shape
    pairs = next(p for p in (WKV_PAIRS, 2, 1) if (d // LANES) % p == 0)
    width = pairs * LANES
    hp = d // width
    nt = t_len // t_blk
    heads = d // RWKV_HEAD
    vec = lambda a: a.reshape(1, d)
    row = lambda off: pl.BlockSpec((t_blk, width), lambda b, p, t, off=off: (b * nt + t, off + p))
    par = pl.BlockSpec((1, width), lambda b, p, t: (0, p))
    st = pl.BlockSpec((1, 2 * pairs, RWKV_HEAD, RWKV_HEAD), lambda b, p, t: (b, p, 0, 0))
    body = functools.partial(_wkv_body, chunk=WKV_CHUNK, n_chunks=t_blk // WKV_CHUNK, pairs=pairs)
    return pl.pallas_call(
        body,
        grid=(batch, hp, nt),
        in_specs=[row(0), row(hp), row(2 * hp), row(3 * hp), row(0), row(0), row(0),
                  par, par, par, par, par, st],
        out_specs=[row(0), st],
        out_shape=[jax.ShapeDtypeStruct((m, d), BF16),
                   jax.ShapeDtypeStruct((batch, heads, RWKV_HEAD, RWKV_HEAD), F32)],
        scratch_shapes=[pltpu.VMEM((pairs, LANES, LANES), F32)],
        compiler_params=_params(("arbitrary", "arbitrary", "arbitrary")),
        name="wkv7_chunked",
    )(rkvg, rkvg, rkvg, rkvg, cum, lw, asig, vec(k_k), vec(k_a), vec(r_k), vec(lnx_w), vec(lnx_b), s0)


def _lambda_val(lam_ref, lam_init):
    lq = lam_ref[...]
    s1 = jnp.sum(lq[0:1] * lq[1:2], axis=-1, keepdims=True)
    s2 = jnp.sum(lq[2:3] * lq[3:4], axis=-1, keepdims=True)
    return jnp.exp(s1) - jnp.exp(s2) + lam_init


def _sub_ln(o, g, lam_init):
    return o * lax.rsqrt(jnp.mean(o * o, axis=-1, keepdims=True) + RMS_EPS) * g * (1.0 - lam_init)


def _head_slope(h, heads):
    return jnp.exp2((-8.0 / heads) * (h + 1).astype(F32))


def _online_softmax_step(s, v_aug, m_ref, l_ref, acc_ref):
    tiles = [s[:, c * LANES:(c + 1) * LANES] for c in range(s.shape[1] // LANES)]
    mt = functools.reduce(jnp.maximum, tiles)
    m_prev = m_ref[...]
    m_next = jnp.maximum(m_prev, jnp.max(mt, axis=-1, keepdims=True))
    p = [jnp.exp2(t - m_next).astype(BF16) for t in tiles]
    p = jnp.concatenate(p, axis=1) if len(p) > 1 else p[0]
    alpha = jnp.exp2(m_prev - m_next)
    pv = jnp.dot(p, v_aug, preferred_element_type=F32)
    acc_ref[...] = alpha * acc_ref[...] + pv[:, :LANES]
    l_ref[...] = alpha * l_ref[...] + pv[:, LANES:]
    m_ref[...] = m_next


def _attn_body(q_ref, k_ref, v_ref, gz_ref, lam_ref, g_ref, o_ref, q2_ref, va_ref, m_ref, l_ref, acc_ref, *,
               tq, wide, heads, lam_init):
    h = pl.program_id(0)
    i = pl.program_id(1)
    m0 = _half_mask()
    slope = _head_slope(jnp.full((1, 1), h, jnp.int32), heads) * LOG2E

    @pl.when(i == 0)
    def _():
        va_ref[:, :LANES] = v_ref[...]
        va_ref[:, LANES:] = jnp.ones(v_ref.shape, BF16)

    q = q_ref[...].astype(F32) * (ATTN_SCALE * LOG2E)
    q2_ref[...] = jnp.concatenate([jnp.where(m0, q, 0.0), jnp.where(m0, 0.0, q)], axis=0).astype(BF16)
    m_ref[...] = jnp.full(m_ref.shape, NEG_INF, F32)
    l_ref[...] = jnp.zeros(l_ref.shape, F32)
    acc_ref[...] = jnp.zeros(acc_ref.shape, F32)

    q_start = i * tq

    def block(k_start, n_sub, diagonal_last):
        sls = [pl.ds(pl.multiple_of(k_start + c * tq, tq), tq) for c in range(n_sub)]
        col = lax.broadcasted_iota(jnp.int32, (1, tq), 1)
        ss = [lax.dot_general(q2_ref[...], k_ref[sl, :], _NT, preferred_element_type=F32) for sl in sls]
        for c in range(n_sub):
            s = ss[c] + slope * (k_start + c * tq - q_start + col).astype(F32)
            if diagonal_last and c == n_sub - 1:
                rr = lax.broadcasted_iota(jnp.int32, (2 * tq, tq), 0) & (tq - 1)
                cc = lax.broadcasted_iota(jnp.int32, (2 * tq, tq), 1)
                s = jnp.where(cc <= rr, s, NEG_INF)
            _online_softmax_step(s, va_ref[sls[c], :], m_ref, l_ref, acc_ref)

    sub_per_wide = wide // tq
    n_wide = q_start // wide
    n_narrow = (q_start - n_wide * wide) // tq

    def wide_block(j, carry):
        block(j * wide, sub_per_wide, False)
        return carry

    lax.fori_loop(0, n_wide, wide_block, 0)
    for v in range(sub_per_wide):
        @pl.when(n_narrow == v)
        def _():
            block(n_wide * wide, v + 1, True)

    lam = _lambda_val(lam_ref, lam_init)
    inv_l = 1.0 / l_ref[...]
    acc = acc_ref[...] * inv_l
    o = acc[:tq] - lam * acc[tq:]
    o_ref[...] = (_sub_ln(o, g_ref[...], lam_init) * _silu(gz_ref[...])).astype(o_ref.dtype)


def _attn_prompt(q, k, v, gz, lam, subln, lam_init, tq):
    t, hd = q.shape
    heads = hd // LANES
    wide = min(4 * tq, t)
    body = functools.partial(_attn_body, tq=tq, wide=wide, heads=heads, lam_init=lam_init)
    return pl.pallas_call(
        body,
        grid=(heads, t // tq),
        in_specs=[pl.BlockSpec((tq, LANES), lambda h, i: (i, h)),
                  pl.BlockSpec((t, LANES), lambda h, i: (0, h)),
                  pl.BlockSpec((t, LANES), lambda h, i: (0, h)),
                  pl.BlockSpec((tq, LANES), lambda h, i: (i, h)),
                  pl.BlockSpec(lam.shape, lambda h, i: (0, 0)),
                  pl.BlockSpec((1, LANES), lambda h, i: (0, 0))],
        out_specs=pl.BlockSpec((tq, LANES), lambda h, i: (i, h)),
        out_shape=jax.ShapeDtypeStruct((t, hd), BF16),
        scratch_shapes=[pltpu.VMEM((2 * tq, LANES), BF16),
                        pltpu.VMEM((t, 2 * LANES), BF16),
                        pltpu.VMEM((2 * tq, LANES), F32),
                        pltpu.VMEM((2 * tq, LANES), F32),
                        pltpu.VMEM((2 * tq, LANES), F32)],
        compiler_params=_params(("arbitrary", "arbitrary")),
        name="diff_attn_prompt",
    )(q, k, v, gz, lam, subln.reshape(1, LANES))


TSLOT = 8
PAGES_PER_STEP = 4
PAGE_GROUPS = 4


def _paged_body(pt_ref, q_ref, *refs, heads, n_pages, t_new, lam_init):
    ck_refs = refs[:PAGES_PER_STEP]
    cv_refs = refs[PAGES_PER_STEP:2 * PAGES_PER_STEP]
    (kn_ref, vn_ref, gz_ref, lam_ref, g_ref, o_ref,
     q2_ref, b0_ref, m_ref, l_ref, acc_ref) = refs[2 * PAGES_PER_STEP:]
    j = pl.program_id(1)
    n_steps = n_pages // PAGES_PER_STEP
    ng = min(PAGE_GROUPS, heads)
    hpg = heads // ng
    hrows = 2 * TSLOT
    grows = hpg * hrows
    rows = heads * hrows
    m0 = _half_mask()
    sh = lambda n: int(math.log2(n))

    ridx = lax.broadcasted_iota(jnp.int32, (rows, 1), 0)
    row_hh = (ridx >> sh(hrows)) & (hpg - 1)
    slope = _head_slope((ridx >> sh(grows)) + ng * row_hh, heads)
    tpos = ridx & (TSLOT - 1)

    def key_layout(width):
        col = lax.broadcasted_iota(jnp.int32, (1, width), 1)
        return col >> sh(hpg), col & (hpg - 1)

    @pl.when((pl.program_id(0) == 0) & (j == 0))
    def _():
        width = PAGES_PER_STEP * PAGE * hpg
        tok, col_hh = key_layout(width)
        dist = (PAGES_PER_STEP * PAGE - tok + tpos).astype(F32)
        b0_ref[...] = jnp.where(col_hh == row_hh, -slope * dist, NEG_INF)

    @pl.when(j == 0)
    def _():
        q = q_ref[0].astype(F32) * ATTN_SCALE
        for h in range(heads):
            qh = q[:, h * LANES:(h + 1) * LANES]
            base = (h % ng) * grows + (h // ng) * hrows
            q2_ref[base:base + hrows, :] = jnp.concatenate(
                [jnp.where(m0, qh, 0.0), jnp.where(m0, 0.0, qh)], axis=0).astype(BF16)
        m_ref[...] = jnp.full(m_ref.shape, NEG_INF, F32)
        l_ref[...] = jnp.zeros(l_ref.shape, F32)
        acc_ref[...] = jnp.zeros(acc_ref.shape, F32)

    def attend(k_tiles, v_tiles, bias):
        def gather(tiles, g):
            parts = [r[0, pl.ds(t * LANES * ng + g, LANES, stride=ng), :] for r, t in tiles]
            x = jnp.concatenate(parts, axis=0) if len(parts) > 1 else parts[0]
            return x.astype(BF16)

        s = jnp.concatenate(
            [lax.dot_general(q2_ref[g * grows:(g + 1) * grows, :], gather(k_tiles, g), _NT,
                             preferred_element_type=F32) for g in range(ng)], axis=0) + bias
        tiles = [s[:, c * LANES:(c + 1) * LANES] for c in range(s.shape[1] // LANES)]
        m_prev = m_ref[...]
        m_next = jnp.maximum(m_prev, jnp.max(functools.reduce(jnp.maximum, tiles), axis=-1, keepdims=True))
        p = [jnp.exp(t - m_next) for t in tiles]
        alpha = jnp.exp(m_prev - m_next)
        l_ref[...] = alpha * l_ref[...] + jnp.sum(functools.reduce(jnp.add, p), axis=-1, keepdims=True)
        p = [t.astype(BF16) for t in p]
        p = jnp.concatenate(p, axis=1) if len(p) > 1 else p[0]
        pv = jnp.concatenate(
            [jnp.dot(p[g * grows:(g + 1) * grows, :], gather(v_tiles, g), preferred_element_type=F32)
             for g in range(ng)], axis=0)
        acc_ref[...] = alpha * acc_ref[...] + pv
        m_ref[...] = m_next

    @pl.when(j < n_steps)
    def _():
        gap = ((n_steps - 1 - j) * (PAGES_PER_STEP * PAGE)).astype(F32)
        attend([(r, t) for r in ck_refs for t in range(hpg)],
               [(r, t) for r in cv_refs for t in range(hpg)],
               b0_ref[...] - slope * gap)

    @pl.when(j == n_steps)
    def _():
        tok, col_hh = key_layout(LANES)
        valid = (col_hh == row_hh) & (tok <= tpos) & (tok < t_new)
        bias = jnp.where(valid, -slope * (tpos - tok).astype(F32), NEG_INF)
        attend([(kn_ref, 0)], [(vn_ref, 0)], bias)
        lam = _lambda_val(lam_ref, lam_init)
        acc = acc_ref[...] * (1.0 / l_ref[...])
        for h in range(heads):
            base = (h % ng) * grows + (h // ng) * hrows
            o = acc[base:base + TSLOT] - lam * acc[base + TSLOT:base + 2 * TSLOT]
            hl = slice(h * LANES, (h + 1) * LANES)
            o_ref[0, :, hl] = _sub_ln(o, g_ref[...], lam_init) * _silu(gz_ref[0, :, hl])


def _attn_paged(q, k_new, v_new, gz, cache_k, cache_v, page_table, lam, subln, lam_init, t_new):
    bsz, _, hd = q.shape
    heads = hd // LANES
    n_pages = page_table.shape[1]
    assert n_pages % PAGES_PER_STEP == 0
    n_steps = n_pages // PAGES_PER_STEP
    rows = heads * 2 * TSLOT
    ng = min(PAGE_GROUPS, heads)
    hpg = heads // ng
    assert heads == ng * hpg and hpg & (hpg - 1) == 0 and t_new <= LANES // hpg and t_new <= TSLOT
    body = functools.partial(_paged_body, heads=heads, n_pages=n_pages, t_new=t_new, lam_init=lam_init)

    def page(a):
        idx = lambda b, j, pt: (pt[b, jnp.minimum(j, n_steps - 1) * PAGES_PER_STEP + a], 0, 0)
        return pl.BlockSpec((1, PAGE * heads, LANES), idx)

    per_b = lambda b, j, pt: (b, 0, 0)
    grid_spec = pltpu.PrefetchScalarGridSpec(
        num_scalar_prefetch=1,
        grid=(bsz, n_steps + 1),
        in_specs=[pl.BlockSpec((1, TSLOT, hd), per_b),
                  *[page(a) for a in range(PAGES_PER_STEP)], *[page(a) for a in range(PAGES_PER_STEP)],
                  pl.BlockSpec((1, LANES * ng, LANES), per_b),
                  pl.BlockSpec((1, LANES * ng, LANES), per_b),
                  pl.BlockSpec((1, TSLOT, hd), per_b),
                  pl.BlockSpec(lam.shape, lambda b, j, pt: (0, 0)),
                  pl.BlockSpec((1, LANES), lambda b, j, pt: (0, 0))],
        out_specs=pl.BlockSpec((1, TSLOT, hd), per_b),
        scratch_shapes=[pltpu.VMEM((rows, LANES), BF16),
                        pltpu.VMEM((rows, PAGES_PER_STEP * PAGE * hpg), F32),
                        pltpu.VMEM((rows, LANES), F32),
                        pltpu.VMEM((rows, LANES), F32),
                        pltpu.VMEM((rows, LANES), F32)])
    return pl.pallas_call(
        body,
        grid_spec=grid_spec,
        out_shape=jax.ShapeDtypeStruct((bsz, TSLOT, hd), F32),
        compiler_params=_params(("arbitrary", "arbitrary")),
        name="diff_attn_paged",
    )(page_table, q, *([cache_k] * PAGES_PER_STEP), *([cache_v] * PAGES_PER_STEP), k_new, v_new, gz, lam,
      subln.reshape(1, LANES))


def _row_tile(m):
    return min(1024, m)


def _rwkv_layer(x, shift, scale, gate, h_prev_row, s0, p, *, batch, t_len):
    m, d = x.shape
    tm = _row_tile(m)
    tn = _col_tile(d)
    h = _modnorm(x, p["norm"], shift, scale, tm)
    h3 = h.reshape(batch, t_len, d)
    hprev = jnp.concatenate([h_prev_row[:, None, :], h3[:, :-1]], axis=1).reshape(m, d)

    rkvg, = _fused_linear("rwkv_in_proj", _mix_prologue, None, [h, hprev], [p["mu"]],
                          p["w_in"], [], [], [F32], tm=tm, tn=tn, groups=4)
    seg = WKV_CHUNK if t_len >= WKV_CHUNK else t_len
    assert t_len % seg == 0
    lw, cum, asig = _lora(h, hprev, p["mu"], p["w0"], p["w1"], p["w2"], p["a0"], p["a1"], p["a2"],
                          min(256, m), seg)

    t_pad = -(-t_len // WKV_CHUNK) * WKV_CHUNK
    if t_pad != t_len:
        def padr(a, mode="constant"):
            return jnp.pad(a.reshape(batch, t_len, -1), ((0, 0), (0, t_pad - t_len), (0, 0)),
                           mode=mode).reshape(batch * t_pad, -1)
        rkvg_w, cum_w, lw_w, asig_w = padr(rkvg), padr(cum, "edge"), padr(lw), padr(asig)
    else:
        rkvg_w, cum_w, lw_w, asig_w = rkvg, cum, lw, asig
    t_blk = min(512, t_pad)
    o_pre, s_fin = _wkv(rkvg_w, cum_w, lw_w, asig_w, p["k_k"], p["k_a"], p["r_k"], p["lnx_w"], p["lnx_b"],
                        s0, batch=batch, t_len=t_pad, t_blk=t_blk)
    if t_pad != t_len:
        o_pre = o_pre.reshape(batch, t_pad, d)[:, :t_len].reshape(m, d)

    x_new, = _fused_linear("rwkv_out_proj", None, _residual_epilogue, [o_pre], [],
                           p["w_out"], [x], [gate], [F32], tm=tm, tn=_wide_col_tile(d))
    return x_new, s_fin, h3[:, -1]


def _tile_gain(gain, n):
    return jnp.tile(gain.reshape(1, LANES), (1, n // LANES))


def _shared_kv(x, shift, scale, p):
    m, d = x.shape
    tm = _row_tile(m)
    qd = p["w_k"].shape[1]
    xn = _modnorm(x, p["norm"], shift, scale, tm, BF16)
    k, k16 = _fused_linear("kv_k_proj", None, _group_rms_epilogue(True), [xn], [], p["w_k"],
                           [], [_tile_gain(p["k_norm"], qd)], [F32, BF16], tm=tm, tn=_wide_col_tile(qd))
    v, v16 = _fused_linear("kv_v_proj", None, None, [xn], [], p["w_v"],
                           [], [], [F32, BF16], tm=tm, tn=_wide_col_tile(p["w_v"].shape[1]))
    return k, v, k16, v16


def _diff_queries(x, shift, scale, p):
    m, d = x.shape
    tm = _row_tile(m)
    qd = p["w_q"].shape[1]
    xn = _modnorm(x, p["norm"], shift, scale, tm, BF16)
    q16, = _fused_linear("diff_q_proj", None, _group_rms_epilogue(False), [xn], [], p["w_q"],
                         [], [_tile_gain(p["q_norm"], qd)], [BF16], tm=tm, tn=_wide_col_tile(qd))
    gz, = _fused_linear("diff_gate_proj", None, None, [xn], [], p["w_z"],
                        [], [], [F32], tm=tm, tn=_wide_col_tile(p["w_z"].shape[1]))
    return q16, gz


def _diff_out(x, o16, gate, w_out):
    m, d = x.shape
    y, = _fused_linear("diff_out_proj", None, _residual_epilogue, [o16], [], w_out,
                       [x], [gate], [F32], tm=_row_tile(m), tn=_wide_col_tile(d))
    return y


def kernel(x_prompt, x_sample, c_prompt, c_sample, state_wkv, state_shift, cache_k, cache_v, page_table,
           a_mod_w, a_mod_b, a_norm, a_mu, a_w_in, a_w0, a_w1, a_w2, a_a0, a_a1, a_a2,
           a_k_k, a_k_a, a_r_k, a_lnx_w, a_lnx_b, a_w_out,
           kv_mod_w, kv_mod_b, kv_norm, kv_w, kv_k_norm,
           b_mod_w, b_mod_b, b_norm, b_w_in, b_q_norm, b_lam, b_subln, b_w_out):
    bp, t_p, d = x_prompt.shape
    bs, t_s, _ = x_sample.shape
    n_a = a_mod_w.shape[0]
    n_b = b_mod_w.shape[0]
    depth = n_a + n_b
    heads_r = d // RWKV_HEAD
    q_dim = b_w_in.shape[2] - d
    lora = a_w1.shape[2]
    lora_pad = -(-lora // LANES) * LANES

    xp = x_prompt.reshape(bp * t_p, d)
    xs = x_sample.reshape(bs * t_s, d)

    n_c = bp + bs
    c_all = jnp.pad(jnp.concatenate([c_prompt, c_sample], axis=0), ((0, -n_c % 8), (0, 0)))

    def mods(w, b, n):
        mv = _mod_vectors(c_all, w, b)
        parts = jnp.split(mv, n, axis=-1)
        assert bp == 1
        prompt = [q[0:1] for q in parts]
        sample = [jnp.repeat(q[bp:n_c], t_s, axis=0) for q in parts]
        return prompt, sample

    wkv_p, wkv_s, shift_p, shift_s = [], [], [], []
    for layer in range(n_a):
        prm = dict(
            norm=a_norm[layer], mu=a_mu[layer],
            w_in=a_w_in[layer].reshape(d, 4 * d).astype(BF16),
            w0=a_w0[layer], a0=a_a0[layer],
            w1=jnp.pad(a_w1[layer], ((0, 0), (0, lora_pad - lora))).astype(BF16),
            w2=jnp.pad(a_w2[layer], ((0, lora_pad - lora), (0, 0))).astype(BF16),
            a1=jnp.pad(a_a1[layer], ((0, 0), (0, lora_pad - lora))).astype(BF16),
            a2=jnp.pad(a_a2[layer], ((0, lora_pad - lora), (0, 0))).astype(BF16),
            k_k=a_k_k[layer], k_a=a_k_a[layer], r_k=a_r_k[layer],
            lnx_w=a_lnx_w[layer], lnx_b=a_lnx_b[layer],
            w_out=a_w_out[layer].astype(BF16))
        (sh_p, sc_p, gt_p), (sh_s, sc_s, gt_s) = mods(a_mod_w[layer], a_mod_b[layer], 3)
        xp, s_fin, last = _rwkv_layer(
            xp, sh_p, sc_p, gt_p, jnp.zeros((bp, d), F32),
            jnp.zeros((bp, heads_r, RWKV_HEAD, RWKV_HEAD), F32), prm, batch=bp, t_len=t_p)
        wkv_p.append(s_fin)
        shift_p.append(last)
        xs, s_fin, last = _rwkv_layer(
            xs, sh_s, sc_s, gt_s, state_shift[layer], state_wkv[layer], prm, batch=bs, t_len=t_s)
        wkv_s.append(s_fin)
        shift_s.append(last)

    (sh_p, sc_p), (sh_s, sc_s) = mods(kv_mod_w, kv_mod_b, 2)
    kvp = dict(norm=kv_norm, w_k=kv_w[:, :q_dim].astype(BF16), w_v=kv_w[:, q_dim:].astype(BF16),
               k_norm=kv_k_norm)
    k_p, v_p, k_p16, v_p16 = _shared_kv(xp, sh_p, sc_p, kvp)
    k_s, v_s, _, _ = _shared_kv(xs, sh_s, sc_s, kvp)

    n_pool = cache_k.shape[0]
    pad_tok = lambda a, n: jnp.pad(a.reshape(bs, t_s, -1), ((0, 0), (0, n - t_s), (0, 0)))
    heads_q = q_dim // LANES
    new_tok = LANES * min(PAGE_GROUPS, heads_q) // heads_q
    as_page = lambda a: pad_tok(a, new_tok).reshape(bs, new_tok * heads_q, LANES)

    for i in range(n_b):
        layer = n_a + i
        lam_init = 0.8 - 0.6 * math.exp(-0.3 * layer)
        qp = dict(norm=b_norm[i], w_q=b_w_in[i][:, :q_dim].astype(BF16),
                  w_z=b_w_in[i][:, q_dim:].astype(BF16), q_norm=b_q_norm[i])
        (sh_p, sc_p, gt_p), (sh_s, sc_s, gt_s) = mods(b_mod_w[i], b_mod_b[i], 3)
        w_out = b_w_out[i].astype(BF16)

        q16, gz = _diff_queries(xp, sh_p, sc_p, qp)
        o16 = _attn_prompt(q16, k_p16, v_p16, gz, b_lam[i], b_subln[i], lam_init, tq=min(256, t_p))
        xp = _diff_out(xp, o16, gt_p, w_out)

        q16, gz = _diff_queries(xs, sh_s, sc_s, qp)
        o = _attn_paged(pad_tok(q16.astype(F32), TSLOT), as_page(k_s), as_page(v_s), pad_tok(gz, TSLOT),
                        cache_k.reshape(n_pool, -1, LANES), cache_v.reshape(n_pool, -1, LANES), page_table,
                        b_lam[i], b_subln[i], lam_init, t_s)
        xs = _diff_out(xs, o[:, :t_s].reshape(bs * t_s, -1).astype(BF16), gt_s, w_out)

    heads_d = q_dim // LANES
    return (xp.reshape(bp, t_p, d), xs.reshape(bs, t_s, d),
            jnp.stack(wkv_p), jnp.stack(shift_p),
            k_p.reshape(bp, t_p, heads_d, LANES), v_p.reshape(bp, t_p, heads_d, LANES),
            jnp.stack(wkv_s), jnp.stack(shift_s),
            k_s.reshape(bs, t_s, heads_d, LANES), v_s.reshape(bs, t_s, heads_d, LANES))
```

```python
import functools
import math

import jax
import jax.numpy as jnp
from jax import lax
from jax.experimental import pallas as pl
from jax.experimental.pallas import tpu as pltpu

F32 = jnp.float32
BF16 = jnp.bfloat16
HIGHEST = lax.Precision.HIGHEST

LANES = 128
RWKV_HEAD = 64
DIFF_HEAD = 64
PAGE = 128
RMS_EPS = 1e-6
LNX_EPS = 64e-5
NEG_INF = -1e30
ATTN_SCALE = DIFF_HEAD ** -0.5
LOG2E = 1.4426950408889634
EPILOGUE_ROWS = 256
WKV_CHUNK = 64
WKV_PAIRS = 8
VMEM_LIMIT = 56 * 1024 * 1024

_NT = (((1,), (1,)), ((), ()))
_TN = (((0,), (0,)), ((), ()))


def _dot(a, b):
    return jnp.dot(a.astype(BF16), b.astype(BF16), preferred_element_type=F32)


def _dot_nt(a, b):
    return lax.dot_general(a.astype(BF16), b.astype(BF16), _NT, preferred_element_type=F32)


def _dot_tn(a, b):
    return lax.dot_general(a.astype(BF16), b.astype(BF16), _TN, preferred_element_type=F32)


def _silu(x):
    return x * (1.0 / (1.0 + jnp.exp(-x)))


def _col_tile(n):
    return next(t for t in (512, 384, 256, 128) if n % t == 0)


def _wide_col_tile(n):
    return next(t for t in (1024, 512, 384, 256, 128) if n % t == 0)


def _params(sem):
    return pltpu.CompilerParams(dimension_semantics=sem, vmem_limit_bytes=VMEM_LIMIT)


def _half_mask():
    return lax.broadcasted_iota(jnp.int32, (1, LANES), 1) < 64


def _group64_sum(x, m0):
    s0 = jnp.sum(jnp.where(m0, x, 0.0), axis=-1, keepdims=True)
    s1 = jnp.sum(jnp.where(m0, 0.0, x), axis=-1, keepdims=True)
    return jnp.where(m0, s0, s1)


def _mod_body(c_ref, w_ref, b_ref, o_ref):
    o_ref[...] = _dot(_silu(c_ref[...]), w_ref[...]) + b_ref[...]


def _mod_vectors(c, w, b):
    m, d = c.shape
    n = w.shape[1]
    tn = _col_tile(n)
    return pl.pallas_call(
        _mod_body,
        grid=(n // tn,),
        in_specs=[pl.BlockSpec((m, d), lambda j: (0, 0)),
                  pl.BlockSpec((d, tn), lambda j: (0, j)),
                  pl.BlockSpec((1, tn), lambda j: (0, j))],
        out_specs=pl.BlockSpec((m, tn), lambda j: (0, j)),
        out_shape=jax.ShapeDtypeStruct((m, n), F32),
        compiler_params=_params(("arbitrary",)),
        name="mod_vectors",
    )(c, w, b.reshape(1, n))


def _modnorm_val(x, g, shift, scale):
    r = lax.rsqrt(jnp.mean(x * x, axis=-1, keepdims=True) + RMS_EPS)
    return (x * r) * g * (1.0 + scale) + shift


def _modnorm_body(x_ref, g_ref, sh_ref, sc_ref, o_ref):
    o_ref[...] = _modnorm_val(x_ref[...], g_ref[...], sh_ref[...], sc_ref[...]).astype(o_ref.dtype)


def _modnorm(x, g, shift, scale, tm, out_dtype=F32):
    m, d = x.shape
    rb = shift.shape[0]
    return pl.pallas_call(
        _modnorm_body,
        grid=(m // tm,),
        in_specs=[pl.BlockSpec((tm, d), lambda i: (i, 0)),
                  pl.BlockSpec((1, d), lambda i: (0, 0)),
                  pl.BlockSpec((rb, d), lambda i: (0, 0)),
                  pl.BlockSpec((rb, d), lambda i: (0, 0))],
        out_specs=pl.BlockSpec((tm, d), lambda i: (i, 0)),
        out_shape=jax.ShapeDtypeStruct((m, d), out_dtype),
        compiler_params=_params(("arbitrary",)),
        name="modnorm",
    )(x, g.reshape(1, d), shift, scale)


def _fused_linear(name, prologue, epilogue, prow, pbc, w, erow, ebc, out_dtypes, *, tm, tn, groups=1):
    prow = [a if isinstance(a, tuple) else (a, 0) for a in prow]
    prow_blk = [cb for _, cb in prow]
    prow = [a for a, _ in prow]
    m = prow[0].shape[0]
    k, n = w.shape
    nj = n // tn
    njg = nj // groups
    rc = min(EPILOGUE_ROWS, tm)
    n_prow, n_pbc, n_erow, n_ebc, n_out = len(prow), len(pbc), len(erow), len(ebc), len(out_dtypes)

    def body(*refs):
        it = iter(refs)
        prow_r = [next(it) for _ in range(n_prow)]
        pbc_r = [next(it) for _ in range(n_pbc)]
        w_r = next(it)
        erow_r = [next(it) for _ in range(n_erow)]
        ebc_r = [next(it) for _ in range(n_ebc)]
        out_r = [next(it) for _ in range(n_out)]
        xs = next(it) if prologue is not None else prow_r[0]
        acc_s = next(it) if epilogue is not None else None
        j = pl.program_id(1)

        if prologue is not None:
            @pl.when(j % njg == 0)
            def _():
                xs[...] = prologue(prow_r, pbc_r, j // njg).astype(BF16)

        acc = jnp.dot(xs[...], w_r[...], preferred_element_type=F32)
        if epilogue is None:
            for o in out_r:
                o[...] = acc.astype(o.dtype)
            return
        acc_s[...] = acc

        def chunk(c, carry):
            rs = pl.ds(pl.multiple_of(c * rc, rc), rc)
            res = epilogue(acc_s[rs, :], [r[rs, :] for r in erow_r],
                           [r[rs, :] if r.shape[0] == tm and tm > 1 else r[...] for r in ebc_r])
            for o, r in zip(out_r, res):
                o[rs, :] = r.astype(o.dtype)
            return carry

        lax.fori_loop(0, tm // rc, chunk, 0)

    in_specs = ([pl.BlockSpec((tm, k), lambda i, j, cb=cb: (i, cb)) for cb in prow_blk]
                + [pl.BlockSpec(a.shape, lambda i, j: (0, 0)) for a in pbc]
                + [pl.BlockSpec((k, tn), lambda i, j: (0, j))]
                + [pl.BlockSpec((tm, tn), lambda i, j: (i, j)) for _ in erow]
                + [pl.BlockSpec((a.shape[0], tn), lambda i, j: (0, j)) for a in ebc])
    outs = pl.pallas_call(
        body,
        grid=(m // tm, nj),
        in_specs=in_specs,
        out_specs=[pl.BlockSpec((tm, tn), lambda i, j: (i, j)) for _ in out_dtypes],
        out_shape=[jax.ShapeDtypeStruct((m, n), dt) for dt in out_dtypes],
        scratch_shapes=([pltpu.VMEM((tm, k), BF16)] if prologue is not None else [])
                       + ([pltpu.VMEM((tm, tn), F32)] if epilogue is not None else []),
        compiler_params=_params(("arbitrary", "arbitrary")),
        name=name,
    )(*prow, *pbc, w, *erow, *ebc)
    return outs


def _residual_epilogue(acc, erows, ebcs):
    return (erows[0] + ebcs[0] * acc,)


def _group_rms_epilogue(with_bf16):
    def epilogue(acc, erows, ebcs):
        m0 = _half_mask()
        gain = ebcs[0]
        cols = []
        for s in range(acc.shape[1] // LANES):
            x = acc[:, s * LANES:(s + 1) * LANES]
            ms = _group64_sum(x * x, m0) * (1.0 / 64.0)
            cols.append(x * lax.rsqrt(ms + RMS_EPS) * gain[:, s * LANES:(s + 1) * LANES])
        y = jnp.concatenate(cols, axis=1) if len(cols) > 1 else cols[0]
        return (y, y) if with_bf16 else (y,)
    return epilogue


def _mix_prologue(prow_r, pbc_r, g):
    h = prow_r[0][...]
    mu = pbc_r[0][pl.ds(g, 1), :]
    return h + (prow_r[1][...] - h) * mu


def _lora_body(h_ref, hp_ref, mu_ref, w0_ref, w1_ref, w2_ref, a0_ref, a1_ref, a2_ref, lw_ref, cum_ref, a_ref,
               *, seg):
    h = h_ref[...]
    dx = hp_ref[...] - h
    xw = h + dx * mu_ref[4:5, :]
    xa = h + dx * mu_ref[5:6, :]
    wl = w0_ref[...] + _dot(jnp.tanh(_dot(xw, w1_ref[...])), w2_ref[...])
    z = -wl
    softplus = jnp.maximum(z, 0.0) + jnp.log(1.0 + jnp.exp(-jnp.abs(z)))
    lw = -jnp.exp(-softplus - 0.5)
    lw_ref[...] = lw
    tm = lw.shape[0]
    row = lax.broadcasted_iota(jnp.int32, (tm, tm), 0)
    col = lax.broadcasted_iota(jnp.int32, (tm, tm), 1)
    seg_shift = int(math.log2(seg))
    tril = ((row >= col) & ((row >> seg_shift) == (col >> seg_shift))).astype(F32)
    cum_ref[...] = jnp.dot(tril, lw, precision=HIGHEST, preferred_element_type=F32)
    al = a0_ref[...] + _dot(_dot(xa, a1_ref[...]), a2_ref[...])
    a_ref[...] = 1.0 / (1.0 + jnp.exp(-al))


def _lora(h, hprev, mu, w0, w1, w2, a0, a1, a2, tm, seg):
    m, d = h.shape
    assert tm % seg == 0 and seg & (seg - 1) == 0
    row = pl.BlockSpec((tm, d), lambda i: (i, 0))
    full = lambda a: pl.BlockSpec(a.shape, lambda i: (0, 0))
    args = (h, hprev, mu, w0.reshape(1, d), w1, w2, a0.reshape(1, d), a1, a2)
    return pl.pallas_call(
        functools.partial(_lora_body, seg=seg),
        grid=(m // tm,),
        in_specs=[row, row] + [full(a) for a in args[2:]],
        out_specs=[row, row, row],
        out_shape=[jax.ShapeDtypeStruct((m, d), F32)] * 3,
        compiler_params=_params(("arbitrary",)),
        name="rwkv_lora",
    )(*args)


def _wkv_body(r_ref, k_ref, v_ref, g_ref, cum_ref, lw_ref, a_ref, kk_ref, ka_ref, rk_ref, lnw_ref, lnb_ref, s0_ref,
              o_ref, sf_ref, z_ref, *, chunk, n_chunks, pairs):
    c2 = 2 * chunk
    t_blk = pl.program_id(2)
    m0 = _half_mask()

    @pl.when(t_blk == 0)
    def _():
        zeros = jnp.zeros((RWKV_HEAD, RWKV_HEAD), F32)
        for pi in range(pairs):
            top = jnp.concatenate([s0_ref[0, 2 * pi], zeros], axis=1)
            bot = jnp.concatenate([zeros, s0_ref[0, 2 * pi + 1]], axis=1)
            z_ref[pi] = jnp.concatenate([top, bot], axis=0)

    row = lax.broadcasted_iota(jnp.int32, (c2, c2), 0)
    col = lax.broadcasted_iota(jnp.int32, (c2, c2), 1)
    strict = row > col
    eye = (row == col).astype(F32)
    incl2 = (lax.broadcasted_iota(jnp.int32, (c2, 2 * c2), 0)
             >= (lax.broadcasted_iota(jnp.int32, (c2, 2 * c2), 1) & (c2 - 1)))
    n_double = int(math.log2(chunk)) - 1

    def stack(x):
        return jnp.concatenate([jnp.where(m0, x, 0.0), jnp.where(m0, 0.0, x)], axis=0)

    prs = range(pairs)
    lanes = [slice(p * LANES, (p + 1) * LANES) for p in prs]

    def step(c, carry):
        sl = pl.ds(pl.multiple_of(c * chunk, chunk), chunk)
        r = [r_ref[sl, l] for l in lanes]
        kr = [k_ref[sl, l] for l in lanes]
        v = [v_ref[sl, l] for l in lanes]
        cum = [cum_ref[sl, l] for l in lanes]
        lw = [lw_ref[sl, l] for l in lanes]
        asg = [a_ref[sl, l] for l in lanes]

        kk = [kr[p] * kk_ref[:, lanes[p]] for p in prs]
        kk = [x / jnp.maximum(jnp.sqrt(_group64_sum(x * x, m0)), 1e-12) for x in kk]
        k = [kr[p] * (1.0 + (asg[p] - 1.0) * ka_ref[:, lanes[p]]) for p in prs]

        cum_end = [x[chunk - 1:chunk, :] for x in cum]
        e_out = [jnp.exp(cum_end[p] - cum[p]) for p in prs]
        a_hat = [-kk[p] * jnp.exp(cum[p] - lw[p] - cum_end[p]) for p in prs]
        r_hat = [r[p] * jnp.exp(cum[p] - cum_end[p]) for p in prs]
        b_hat = [kk[p] * asg[p] * e_out[p] for p in prs]
        k_hat = [k[p] * e_out[p] for p in prs]

        ar = [jnp.concatenate([stack(a_hat[p]), stack(r_hat[p])], axis=0) for p in prs]
        bk = [jnp.concatenate([stack(b_hat[p]), stack(k_hat[p])], axis=0) for p in prs]
        v_st = [stack(x) for x in v]

        mm = [_dot_nt(ar[p], bk[p]) for p in prs]
        l_ab = [jnp.where(strict, x[:c2, :c2], 0.0) for x in mm]
        l_ak = [jnp.where(strict, x[:c2, c2:], 0.0) for x in mm]
        l_r = [jnp.where(incl2, x[c2:, :], 0.0) for x in mm]

        t_inv = [eye + x for x in l_ab]
        l_pow = l_ab
        for _ in range(n_double):
            l_pow = [_dot(x, x) for x in l_pow]
            t_inv = [t + _dot(x, t) for x, t in zip(l_pow, t_inv)]

        lv = [_dot(l_ak[p], v_st[p]) for p in prs]
        zd = [z_ref[p] * jnp.exp(cum_end[p]) for p in prs]
        pz = [_dot_nt(ar[p], zd[p]) for p in prs]
        u = [_dot(t_inv[p], pz[p][:c2] + lv[p]) for p in prs]
        uv = [jnp.concatenate([u[p], v_st[p]], axis=0) for p in prs]
        y_st = [pz[p][c2:] + _dot(l_r[p], uv[p]) for p in prs]
        for p in prs:
            z_ref[p] = zd[p] + _dot_tn(uv[p], bk[p])

        for p in prs:
            y = y_st[p][:chunk] + y_st[p][chunk:]
            mean = _group64_sum(y, m0) * (1.0 / 64.0)
            yc = y - mean
            var = _group64_sum(yc * yc, m0) * (1.0 / 64.0)
            bonus = _group64_sum(r[p] * k[p] * rk_ref[:, lanes[p]], m0) * v[p]
            o = yc * lax.rsqrt(var + LNX_EPS) * lnw_ref[:, lanes[p]] + lnb_ref[:, lanes[p]] + bonus
            o_ref[sl, lanes[p]] = (o * _silu(g_ref[sl, lanes[p]])).astype(o_ref.dtype)
        return carry

    lax.fori_loop(0, n_chunks, step, 0)

    @pl.when(t_blk == pl.num_programs(2) - 1)
    def _():
        for pi in range(pairs):
            z = z_ref[pi]
            sf_ref[0, 2 * pi] = z[:RWKV_HEAD, :RWKV_HEAD]
            sf_ref[0, 2 * pi + 1] = z[RWKV_HEAD:, RWKV_HEAD:]


def _wkv(rkvg, cum, lw, asig, k_k, k_a, r_k, lnx_w, lnx_b, s0, *, batch, t_len, t_blk):
    m, d = lw.shape
    pairs = next(p for p in (WKV_PAIRS, 2, 1) if (d // LANES) % p == 0)
    width = pairs * LANES
    hp = d // width
    nt = t_len // t_blk
    heads = d // RWKV_HEAD
    vec = lambda a: a.reshape(1, d)
    row = lambda off: pl.BlockSpec((t_blk, width), lambda b, p, t, off=off: (b * nt + t, off + p))
    par = pl.BlockSpec((1, width), lambda b, p, t: (0, p))
    st = pl.BlockSpec((1, 2 * pairs, RWKV_HEAD, RWKV_HEAD), lambda b, p, t: (b, p, 0, 0))
    body = functools.partial(_wkv_body, chunk=WKV_CHUNK, n_chunks=t_blk // WKV_CHUNK, pairs=pairs)
    return pl.pallas_call(
        body,
        grid=(batch, hp, nt),
        in_specs=[row(0), row(hp), row(2 * hp), row(3 * hp), row(0), row(0), row(0),
                  par, par, par, par, par, st],
        out_specs=[row(0), st],
        out_shape=[jax.ShapeDtypeStruct((m, d), BF16),
                   jax.ShapeDtypeStruct((batch, heads, RWKV_HEAD, RWKV_HEAD), F32)],
        scratch_shapes=[pltpu.VMEM((pairs, LANES, LANES), F32)],
        compiler_params=_params(("arbitrary", "arbitrary", "arbitrary")),
        name="wkv7_chunked",
    )(rkvg, rkvg, rkvg, rkvg, cum, lw, asig, vec(k_k), vec(k_a), vec(r_k), vec(lnx_w), vec(lnx_b), s0)


def _lambda_val(lam_ref, lam_init):
    lq = lam_ref[...]
    s1 = jnp.sum(lq[0:1] * lq[1:2], axis=-1, keepdims=True)
    s2 = jnp.sum(lq[2:3] * lq[3:4], axis=-1, keepdims=True)
    return jnp.exp(s1) - jnp.exp(s2) + lam_init


def _sub_ln(o, g, lam_init):
    return o * lax.rsqrt(jnp.mean(o * o, axis=-1, keepdims=True) + RMS_EPS) * g * (1.0 - lam_init)


def _head_slope(h, heads):
    return jnp.exp2((-8.0 / heads) * (h + 1).astype(F32))


def _online_softmax_step(s, v_aug, m_ref, l_ref, acc_ref):
    tiles = [s[:, c * LANES:(c + 1) * LANES] for c in range(s.shape[1] // LANES)]
    mt = functools.reduce(jnp.maximum, tiles)
    m_prev = m_ref[...]
    m_next = jnp.maximum(m_prev, jnp.max(mt, axis=-1, keepdims=True))
    p = [jnp.exp2(t - m_next).astype(BF16) for t in tiles]
    p = jnp.concatenate(p, axis=1) if len(p) > 1 else p[0]
    alpha = jnp.exp2(m_prev - m_next)
    pv = jnp.dot(p, v_aug, preferred_element_type=F32)
    acc_ref[...] = alpha * acc_ref[...] + pv[:, :LANES]
    l_ref[...] = alpha * l_ref[...] + pv[:, LANES:]
    m_ref[...] = m_next


def _attn_body(q_ref, k_ref, v_ref, gz_ref, lam_ref, g_ref, o_ref, q2_ref, va_ref, m_ref, l_ref, acc_ref, *,
               tq, wide, heads, lam_init):
    h = pl.program_id(0)
    i = pl.program_id(1)
    m0 = _half_mask()
    slope = _head_slope(jnp.full((1, 1), h, jnp.int32), heads) * LOG2E

    @pl.when(i == 0)
    def _():
        va_ref[:, :LANES] = v_ref[...]
        va_ref[:, LANES:] = jnp.ones(v_ref.shape, BF16)

    q = q_ref[...].astype(F32) * (ATTN_SCALE * LOG2E)
    q2_ref[...] = jnp.concatenate([jnp.where(m0, q, 0.0), jnp.where(m0, 0.0, q)], axis=0).astype(BF16)
    m_ref[...] = jnp.full(m_ref.shape, NEG_INF, F32)
    l_ref[...] = jnp.zeros(l_ref.shape, F32)
    acc_ref[...] = jnp.zeros(acc_ref.shape, F32)

    q_start = i * tq

    def block(k_start, n_sub, diagonal_last):
        sls = [pl.ds(pl.multiple_of(k_start + c * tq, tq), tq) for c in range(n_sub)]
        col = lax.broadcasted_iota(jnp.int32, (1, tq), 1)
        ss = [lax.dot_general(q2_ref[...], k_ref[sl, :], _NT, preferred_element_type=F32) for sl in sls]
        for c in range(n_sub):
            s = ss[c] + slope * (k_start + c * tq - q_start + col).astype(F32)
            if diagonal_last and c == n_sub - 1:
                rr = lax.broadcasted_iota(jnp.int32, (2 * tq, tq), 0) & (tq - 1)
                cc = lax.broadcasted_iota(jnp.int32, (2 * tq, tq), 1)
                s = jnp.where(cc <= rr, s, NEG_INF)
            _online_softmax_step(s, va_ref[sls[c], :], m_ref, l_ref, acc_ref)

    sub_per_wide = wide // tq
    n_wide = q_start // wide
    n_narrow = (q_start - n_wide * wide) // tq

    def wide_block(j, carry):
        block(j * wide, sub_per_wide, False)
        return carry

    lax.fori_loop(0, n_wide, wide_block, 0)
    for v in range(sub_per_wide):
        @pl.when(n_narrow == v)
        def _():
            block(n_wide * wide, v + 1, True)

    lam = _lambda_val(lam_ref, lam_init)
    inv_l = 1.0 / l_ref[...]
    acc = acc_ref[...] * inv_l
    o = acc[:tq] - lam * acc[tq:]
    o_ref[...] = (_sub_ln(o, g_ref[...], lam_init) * _silu(gz_ref[...])).astype(o_ref.dtype)


def _attn_prompt(q, k, v, gz, lam, subln, lam_init, tq):
    t, hd = q.shape
    heads = hd // LANES
    wide = min(4 * tq, t)
    body = functools.partial(_attn_body, tq=tq, wide=wide, heads=heads, lam_init=lam_init)
    return pl.pallas_call(
        body,
        grid=(heads, t // tq),
        in_specs=[pl.BlockSpec((tq, LANES), lambda h, i: (i, h)),
                  pl.BlockSpec((t, LANES), lambda h, i: (0, h)),
                  pl.BlockSpec((t, LANES), lambda h, i: (0, h)),
                  pl.BlockSpec((tq, LANES), lambda h, i: (i, h)),
                  pl.BlockSpec(lam.shape, lambda h, i: (0, 0)),
                  pl.BlockSpec((1, LANES), lambda h, i: (0, 0))],
        out_specs=pl.BlockSpec((tq, LANES), lambda h, i: (i, h)),
        out_shape=jax.ShapeDtypeStruct((t, hd), BF16),
        scratch_shapes=[pltpu.VMEM((2 * tq, LANES), BF16),
                        pltpu.VMEM((t, 2 * LANES), BF16),
                        pltpu.VMEM((2 * tq, LANES), F32),
                        pltpu.VMEM((2 * tq, LANES), F32),
                        pltpu.VMEM((2 * tq, LANES), F32)],
        compiler_params=_params(("arbitrary", "arbitrary")),
        name="diff_attn_prompt",
    )(q, k, v, gz, lam, subln.reshape(1, LANES))


TSLOT = 8
PAGES_PER_STEP = 8
PAGE_GROUPS = 4


def _paged_body(pt_ref, q_ref, *refs, heads, n_pages, t_new, lam_init):
    ck_refs = refs[:PAGES_PER_STEP]
    cv_refs = refs[PAGES_PER_STEP:2 * PAGES_PER_STEP]
    (kn_ref, vn_ref, gz_ref, lam_ref, g_ref, o_ref,
     q2_ref, b0_ref, m_ref, l_ref, acc_ref) = refs[2 * PAGES_PER_STEP:]
    j = pl.program_id(1)
    n_steps = n_pages // PAGES_PER_STEP
    ng = min(PAGE_GROUPS, heads)
    hpg = heads // ng
    hrows = 2 * TSLOT
    grows = hpg * hrows
    rows = heads * hrows
    m0 = _half_mask()
    sh = lambda n: int(math.log2(n))

    ridx = lax.broadcasted_iota(jnp.int32, (rows, 1), 0)
    row_hh = (ridx >> sh(hrows)) & (hpg - 1)
    slope = _head_slope((ridx >> sh(grows)) + ng * row_hh, heads)
    tpos = ridx & (TSLOT - 1)

    def key_layout(width):
        col = lax.broadcasted_iota(jnp.int32, (1, width), 1)
        return col >> sh(hpg), col & (hpg - 1)

    @pl.when((pl.program_id(0) == 0) & (j == 0))
    def _():
        width = PAGES_PER_STEP * PAGE * hpg
        tok, col_hh = key_layout(width)
        dist = (PAGES_PER_STEP * PAGE - tok + tpos).astype(F32)
        b0_ref[...] = jnp.where(col_hh == row_hh, -slope * dist, NEG_INF)

    @pl.when(j == 0)
    def _():
        q = q_ref[0].astype(F32) * ATTN_SCALE
        for h in range(heads):
            qh = q[:, h * LANES:(h + 1) * LANES]
            base = (h % ng) * grows + (h // ng) * hrows
            q2_ref[base:base + hrows, :] = jnp.concatenate(
                [jnp.where(m0, qh, 0.0), jnp.where(m0, 0.0, qh)], axis=0).astype(BF16)
        m_ref[...] = jnp.full(m_ref.shape, NEG_INF, F32)
        l_ref[...] = jnp.zeros(l_ref.shape, F32)
        acc_ref[...] = jnp.zeros(acc_ref.shape, F32)

    def attend(k_tiles, v_tiles, bias):
        def gather(tiles, g):
            parts = [r[0, pl.ds(t * LANES * ng + g, LANES, stride=ng), :] for r, t in tiles]
            x = jnp.concatenate(parts, axis=0) if len(parts) > 1 else parts[0]
            return x.astype(BF16)

        s = jnp.concatenate(
            [lax.dot_general(q2_ref[g * grows:(g + 1) * grows, :], gather(k_tiles, g), _NT,
                             preferred_element_type=F32) for g in range(ng)], axis=0) + bias
        tiles = [s[:, c * LANES:(c + 1) * LANES] for c in range(s.shape[1] // LANES)]
        m_prev = m_ref[...]
        m_next = jnp.maximum(m_prev, jnp.max(functools.reduce(jnp.maximum, tiles), axis=-1, keepdims=True))
        p = [jnp.exp(t - m_next) for t in tiles]
        alpha = jnp.exp(m_prev - m_next)
        l_ref[...] = alpha * l_ref[...] + jnp.sum(functools.reduce(jnp.add, p), axis=-1, keepdims=True)
        p = [t.astype(BF16) for t in p]
        p = jnp.concatenate(p, axis=1) if len(p) > 1 else p[0]
        pv = jnp.concatenate(
            [jnp.dot(p[g * grows:(g + 1) * grows, :], gather(v_tiles, g), preferred_element_type=F32)
             for g in range(ng)], axis=0)
        acc_ref[...] = alpha * acc_ref[...] + pv
        m_ref[...] = m_next

    @pl.when(j < n_steps)
    def _():
        gap = ((n_steps - 1 - j) * (PAGES_PER_STEP * PAGE)).astype(F32)
        attend([(r, t) for r in ck_refs for t in range(hpg)],
               [(r, t) for r in cv_refs for t in range(hpg)],
               b0_ref[...] - slope * gap)

    @pl.when(j == n_steps)
    def _():
        tok, col_hh = key_layout(LANES)
        valid = (col_hh == row_hh) & (tok <= tpos) & (tok < t_new)
        bias = jnp.where(valid, -slope * (tpos - tok).astype(F32), NEG_INF)
        attend([(kn_ref, 0)], [(vn_ref, 0)], bias)
        lam = _lambda_val(lam_ref, lam_init)
        acc = acc_ref[...] * (1.0 / l_ref[...])
        for h in range(heads):
            base = (h % ng) * grows + (h // ng) * hrows
            o = acc[base:base + TSLOT] - lam * acc[base + TSLOT:base + 2 * TSLOT]
            hl = slice(h * LANES, (h + 1) * LANES)
            o_ref[0, :, hl] = _sub_ln(o, g_ref[...], lam_init) * _silu(gz_ref[0, :, hl])


def _attn_paged(q, k_new, v_new, gz, cache_k, cache_v, page_table, lam, subln, lam_init, t_new):
    bsz, _, hd = q.shape
    heads = hd // LANES
    n_pages = page_table.shape[1]
    assert n_pages % PAGES_PER_STEP == 0
    n_steps = n_pages // PAGES_PER_STEP
    rows = heads * 2 * TSLOT
    ng = min(PAGE_GROUPS, heads)
    hpg = heads // ng
    assert heads == ng * hpg and hpg & (hpg - 1) == 0 and t_new <= LANES // hpg and t_new <= TSLOT
    body = functools.partial(_paged_body, heads=heads, n_pages=n_pages, t_new=t_new, lam_init=lam_init)

    def page(a):
        idx = lambda b, j, pt: (pt[b, jnp.minimum(j, n_steps - 1) * PAGES_PER_STEP + a], 0, 0)
        return pl.BlockSpec((1, PAGE * heads, LANES), idx)

    per_b = lambda b, j, pt: (b, 0, 0)
    grid_spec = pltpu.PrefetchScalarGridSpec(
        num_scalar_prefetch=1,
        grid=(bsz, n_steps + 1),
        in_specs=[pl.BlockSpec((1, TSLOT, hd), per_b),
                  *[page(a) for a in range(PAGES_PER_STEP)], *[page(a) for a in range(PAGES_PER_STEP)],
                  pl.BlockSpec((1, LANES * ng, LANES), per_b),
                  pl.BlockSpec((1, LANES * ng, LANES), per_b),
                  pl.BlockSpec((1, TSLOT, hd), per_b),
                  pl.BlockSpec(lam.shape, lambda b, j, pt: (0, 0)),
                  pl.BlockSpec((1, LANES), lambda b, j, pt: (0, 0))],
        out_specs=pl.BlockSpec((1, TSLOT, hd), per_b),
        scratch_shapes=[pltpu.VMEM((rows, LANES), BF16),
                        pltpu.VMEM((rows, PAGES_PER_STEP * PAGE * hpg), F32),
                        pltpu.VMEM((rows, LANES), F32),
                        pltpu.VMEM((rows, LANES), F32),
                        pltpu.VMEM((rows, LANES), F32)])
    return pl.pallas_call(
        body,
        grid_spec=grid_spec,
        out_shape=jax.ShapeDtypeStruct((bsz, TSLOT, hd), F32),
        compiler_params=_params(("arbitrary", "arbitrary")),
        name="diff_attn_paged",
    )(page_table, q, *([cache_k] * PAGES_PER_STEP), *([cache_v] * PAGES_PER_STEP), k_new, v_new, gz, lam,
      subln.reshape(1, LANES))


def _row_tile(m):
    return min(1024, m)


def _rwkv_layer(x, shift, scale, gate, h_prev_row, s0, p, *, batch, t_len):
    m, d = x.shape
    tm = _row_tile(m)
    tn = _col_tile(d)
    h = _modnorm(x, p["norm"], shift, scale, tm)
    h3 = h.reshape(batch, t_len, d)
    hprev = jnp.concatenate([h_prev_row[:, None, :], h3[:, :-1]], axis=1).reshape(m, d)

    rkvg, = _fused_linear("rwkv_in_proj", _mix_prologue, None, [h, hprev], [p["mu"]],
                          p["w_in"], [], [], [F32], tm=tm, tn=tn, groups=4)
    seg = WKV_CHUNK if t_len >= WKV_CHUNK else t_len
    assert t_len % seg == 0
    lw, cum, asig = _lora(h, hprev, p["mu"], p["w0"], p["w1"], p["w2"], p["a0"], p["a1"], p["a2"],
                          min(256, m), seg)

    t_pad = -(-t_len // WKV_CHUNK) * WKV_CHUNK
    if t_pad != t_len:
        def padr(a, mode="constant"):
            return jnp.pad(a.reshape(batch, t_len, -1), ((0, 0), (0, t_pad - t_len), (0, 0)),
                           mode=mode).reshape(batch * t_pad, -1)
        rkvg_w, cum_w, lw_w, asig_w = padr(rkvg), padr(cum, "edge"), padr(lw), padr(asig)
    else:
        rkvg_w, cum_w, lw_w, asig_w = rkvg, cum, lw, asig
    t_blk = min(512, t_pad)
    o_pre, s_fin = _wkv(rkvg_w, cum_w, lw_w, asig_w, p["k_k"], p["k_a"], p["r_k"], p["lnx_w"], p["lnx_b"],
                        s0, batch=batch, t_len=t_pad, t_blk=t_blk)
    if t_pad != t_len:
        o_pre = o_pre.reshape(batch, t_pad, d)[:, :t_len].reshape(m, d)

    x_new, = _fused_linear("rwkv_out_proj", None, _residual_epilogue, [o_pre], [],
                           p["w_out"], [x], [gate], [F32], tm=tm, tn=_wide_col_tile(d))
    return x_new, s_fin, h3[:, -1]


def _tile_gain(gain, n):
    return jnp.tile(gain.reshape(1, LANES), (1, n // LANES))


def _shared_kv(x, shift, scale, p):
    m, d = x.shape
    tm = _row_tile(m)
    qd = p["w_k"].shape[1]
    xn = _modnorm(x, p["norm"], shift, scale, tm, BF16)
    k, k16 = _fused_linear("kv_k_proj", None, _group_rms_epilogue(True), [xn], [], p["w_k"],
                           [], [_tile_gain(p["k_norm"], qd)], [F32, BF16], tm=tm, tn=_wide_col_tile(qd))
    v, v16 = _fused_linear("kv_v_proj", None, None, [xn], [], p["w_v"],
                           [], [], [F32, BF16], tm=tm, tn=_wide_col_tile(p["w_v"].shape[1]))
    return k, v, k16, v16


def _diff_queries(x, shift, scale, p):
    m, d = x.shape
    tm = _row_tile(m)
    qd = p["w_q"].shape[1]
    xn = _modnorm(x, p["norm"], shift, scale, tm, BF16)
    q16, = _fused_linear("diff_q_proj", None, _group_rms_epilogue(False), [xn], [], p["w_q"],
                         [], [_tile_gain(p["q_norm"], qd)], [BF16], tm=tm, tn=_wide_col_tile(qd))
    gz, = _fused_linear("diff_gate_proj", None, None, [xn], [], p["w_z"],
                        [], [], [F32], tm=tm, tn=_wide_col_tile(p["w_z"].shape[1]))
    return q16, gz


def _diff_out(x, o16, gate, w_out):
    m, d = x.shape
    y, = _fused_linear("diff_out_proj", None, _residual_epilogue, [o16], [], w_out,
                       [x], [gate], [F32], tm=_row_tile(m), tn=_wide_col_tile(d))
    return y


def kernel(x_prompt, x_sample, c_prompt, c_sample, state_wkv, state_shift, cache_k, cache_v, page_table,
           a_mod_w, a_mod_b, a_norm, a_mu, a_w_in, a_w0, a_w1, a_w2, a_a0, a_a1, a_a2,
           a_k_k, a_k_a, a_r_k, a_lnx_w, a_lnx_b, a_w_out,
           kv_mod_w, kv_mod_b, kv_norm, kv_w, kv_k_norm,
           b_mod_w, b_mod_b, b_norm, b_w_in, b_q_norm, b_lam, b_subln, b_w_out):
    bp, t_p, d = x_prompt.shape
    bs, t_s, _ = x_sample.shape
    n_a = a_mod_w.shape[0]
    n_b = b_mod_w.shape[0]
    depth = n_a + n_b
    heads_r = d // RWKV_HEAD
    q_dim = b_w_in.shape[2] - d
    lora = a_w1.shape[2]
    lora_pad = -(-lora // LANES) * LANES

    xp = x_prompt.reshape(bp * t_p, d)
    xs = x_sample.reshape(bs * t_s, d)

    n_c = bp + bs
    c_all = jnp.pad(jnp.concatenate([c_prompt, c_sample], axis=0), ((0, -n_c % 16), (0, 0)))

    def mods(w, b, n):
        mv = _mod_vectors(c_all, w, b)
        parts = jnp.split(mv, n, axis=-1)
        assert bp == 1
        prompt = [q[0:1] for q in parts]
        sample = [jnp.repeat(q[bp:n_c], t_s, axis=0) for q in parts]
        return prompt, sample

    wkv_p, wkv_s, shift_p, shift_s = [], [], [], []
    for layer in range(n_a):
        prm = dict(
            norm=a_norm[layer], mu=a_mu[layer],
            w_in=a_w_in[layer].reshape(d, 4 * d).astype(BF16),
            w0=a_w0[layer], a0=a_a0[layer],
            w1=jnp.pad(a_w1[layer], ((0, 0), (0, lora_pad - lora))).astype(BF16),
            w2=jnp.pad(a_w2[layer], ((0, lora_pad - lora), (0, 0))).astype(BF16),
            a1=jnp.pad(a_a1[layer], ((0, 0), (0, lora_pad - lora))).astype(BF16),
            a2=jnp.pad(a_a2[layer], ((0, lora_pad - lora), (0, 0))).astype(BF16),
            k_k=a_k_k[layer], k_a=a_k_a[layer], r_k=a_r_k[layer],
            lnx_w=a_lnx_w[layer], lnx_b=a_lnx_b[layer],
            w_out=a_w_out[layer].astype(BF16))
        (sh_p, sc_p, gt_p), (sh_s, sc_s, gt_s) = mods(a_mod_w[layer], a_mod_b[layer], 3)
        xp, s_fin, last = _rwkv_layer(
            xp, sh_p, sc_p, gt_p, jnp.zeros((bp, d), F32),
            jnp.zeros((bp, heads_r, RWKV_HEAD, RWKV_HEAD), F32), prm, batch=bp, t_len=t_p)
        wkv_p.append(s_fin)
        shift_p.append(last)
        xs, s_fin, last = _rwkv_layer(
            xs, sh_s, sc_s, gt_s, state_shift[layer], state_wkv[layer], prm, batch=bs, t_len=t_s)
        wkv_s.append(s_fin)
        shift_s.append(last)

    (sh_p, sc_p), (sh_s, sc_s) = mods(kv_mod_w, kv_mod_b, 2)
    kvp = dict(norm=kv_norm, w_k=kv_w[:, :q_dim].astype(BF16), w_v=kv_w[:, q_dim:].astype(BF16),
               k_norm=kv_k_norm)
    k_p, v_p, k_p16, v_p16 = _shared_kv(xp, sh_p, sc_p, kvp)
    k_s, v_s, _, _ = _shared_kv(xs, sh_s, sc_s, kvp)

    n_pool = cache_k.shape[0]
    pad_tok = lambda a, n: jnp.pad(a.reshape(bs, t_s, -1), ((0, 0), (0, n - t_s), (0, 0)))
    heads_q = q_dim // LANES
    new_tok = LANES * min(PAGE_GROUPS, heads_q) // heads_q
    as_page = lambda a: pad_tok(a, new_tok).reshape(bs, new_tok * heads_q, LANES)

    for i in range(n_b):
        layer = n_a + i
        lam_init = 0.8 - 0.6 * math.exp(-0.3 * layer)
        qp = dict(norm=b_norm[i], w_q=b_w_in[i][:, :q_dim].astype(BF16),
                  w_z=b_w_in[i][:, q_dim:].astype(BF16), q_norm=b_q_norm[i])
        (sh_p, sc_p, gt_p), (sh_s, sc_s, gt_s) = mods(b_mod_w[i], b_mod_b[i], 3)
        w_out = b_w_out[i].astype(BF16)

        q16, gz = _diff_queries(xp, sh_p, sc_p, qp)
        o16 = _attn_prompt(q16, k_p16, v_p16, gz, b_lam[i], b_subln[i], lam_init, tq=min(512, t_p))
        xp = _diff_out(xp, o16, gt_p, w_out)

        q16, gz = _diff_queries(xs, sh_s, sc_s, qp)
        o = _attn_paged(pad_tok(q16.astype(F32), TSLOT), as_page(k_s), as_page(v_s), pad_tok(gz, TSLOT),
                        cache_k.reshape(n_pool, -1, LANES), cache_v.reshape(n_pool, -1, LANES), page_table,
                        b_lam[i], b_subln[i], lam_init, t_s)
        xs = _diff_out(xs, o[:, :t_s].reshape(bs * t_s, -1).astype(BF16), gt_s, w_out)

    heads_d = q_dim // LANES
    return (xp.reshape(bp, t_p, d), xs.reshape(bs, t_s, d),
            jnp.stack(wkv_p), jnp.stack(shift_p),
            k_p.reshape(bp, t_p, heads_d, LANES), v_p.reshape(bp, t_p, heads_d, LANES),
            jnp.stack(wkv_s), jnp.stack(shift_s),
            k_s.reshape(bs, t_s, heads_d, LANES), v_s.reshape(bs, t_s, heads_d, LANES))
```

```python
import functools
import math

import jax
import jax.numpy as jnp
from jax import lax
from jax.experimental import pallas as pl
from jax.experimental.pallas import tpu as pltpu

F32 = jnp.float32
BF16 = jnp.bfloat16

LANES = 128
RWKV_HEAD = 64
DIFF_HEAD = 64
PAGE = 128
RMS_EPS = 1e-6
LNX_EPS = 64e-5
NEG_INF = -1e30
ATTN_SCALE = DIFF_HEAD ** -0.5
LOG2E = 1.4426950408889634
EPILOGUE_ROWS = 256
WKV_CHUNK = 64
WKV_PAIRS = 16
VMEM_LIMIT = 56 * 1024 * 1024

_NT = (((1,), (1,)), ((), ()))
_TN = (((0,), (0,)), ((), ()))


def _dot(a, b):
    return jnp.dot(a.astype(BF16), b.astype(BF16), preferred_element_type=F32)


def _dot_nt(a, b):
    return lax.dot_general(a.astype(BF16), b.astype(BF16), _NT, preferred_element_type=F32)


def _dot_tn(a, b):
    return lax.dot_general(a.astype(BF16), b.astype(BF16), _TN, preferred_element_type=F32)


def _silu(x):
    return x * (1.0 / (1.0 + jnp.exp(-x)))


def _col_tile(n):
    return next(t for t in (512, 384, 256, 128) if n % t == 0)


def _wide_col_tile(n):
    return next(t for t in (1024, 512, 384, 256, 128) if n % t == 0)


def _params(sem):
    return pltpu.CompilerParams(dimension_semantics=sem, vmem_limit_bytes=VMEM_LIMIT)


def _half_mask():
    return lax.broadcasted_iota(jnp.int32, (1, LANES), 1) < 64


def _group64_sum(x, m0):
    s0 = jnp.sum(jnp.where(m0, x, 0.0), axis=-1, keepdims=True)
    s1 = jnp.sum(jnp.where(m0, 0.0, x), axis=-1, keepdims=True)
    return jnp.where(m0, s0, s1)


def _mod_body(c_ref, w_ref, b_ref, o_ref):
    o_ref[...] = _dot(_silu(c_ref[...]), w_ref[...]) + b_ref[...]


def _mod_vectors(c, w, b):
    m, d = c.shape
    n = w.shape[1]
    tn = _col_tile(n)
    return pl.pallas_call(
        _mod_body,
        grid=(n // tn,),
        in_specs=[pl.BlockSpec((m, d), lambda j: (0, 0)),
                  pl.BlockSpec((d, tn), lambda j: (0, j)),
                  pl.BlockSpec((1, tn), lambda j: (0, j))],
        out_specs=pl.BlockSpec((m, tn), lambda j: (0, j)),
        out_shape=jax.ShapeDtypeStruct((m, n), F32),
        compiler_params=_params(("arbitrary",)),
        name="mod_vectors",
    )(c, w, b.reshape(1, n))


def _modnorm_val(x, g, shift, scale):
    r = lax.rsqrt(jnp.mean(x * x, axis=-1, keepdims=True) + RMS_EPS)
    return (x * r) * g * (1.0 + scale) + shift


def _modnorm_body(x_ref, g_ref, sh_ref, sc_ref, o_ref):
    o_ref[...] = _modnorm_val(x_ref[...], g_ref[...], sh_ref[...], sc_ref[...]).astype(o_ref.dtype)


def _modnorm(x, g, shift, scale, tm, out_dtype=F32):
    m, d = x.shape
    rb = shift.shape[0]
    return pl.pallas_call(
        _modnorm_body,
        grid=(m // tm,),
        in_specs=[pl.BlockSpec((tm, d), lambda i: (i, 0)),
                  pl.BlockSpec((1, d), lambda i: (0, 0)),
                  pl.BlockSpec((rb, d), lambda i: (0, 0)),
                  pl.BlockSpec((rb, d), lambda i: (0, 0))],
        out_specs=pl.BlockSpec((tm, d), lambda i: (i, 0)),
        out_shape=jax.ShapeDtypeStruct((m, d), out_dtype),
        compiler_params=_params(("arbitrary",)),
        name="modnorm",
    )(x, g.reshape(1, d), shift, scale)


def _fused_linear(name, prologue, epilogue, prow, pbc, w, erow, ebc, out_dtypes, *, tm, tn, groups=1):
    prow = [a if isinstance(a, tuple) else (a, 0) for a in prow]
    prow_blk = [cb for _, cb in prow]
    prow = [a for a, _ in prow]
    m = prow[0].shape[0]
    k, n = w.shape
    nj = n // tn
    njg = nj // groups
    rc = min(EPILOGUE_ROWS, tm)
    n_prow, n_pbc, n_erow, n_ebc, n_out = len(prow), len(pbc), len(erow), len(ebc), len(out_dtypes)

    def body(*refs):
        it = iter(refs)
        prow_r = [next(it) for _ in range(n_prow)]
        pbc_r = [next(it) for _ in range(n_pbc)]
        w_r = next(it)
        erow_r = [next(it) for _ in range(n_erow)]
        ebc_r = [next(it) for _ in range(n_ebc)]
        out_r = [next(it) for _ in range(n_out)]
        xs = next(it) if prologue is not None else prow_r[0]
        acc_s = next(it) if epilogue is not None else None
        j = pl.program_id(1)

        if prologue is not None:
            @pl.when(j % njg == 0)
            def _():
                xs[...] = prologue(prow_r, pbc_r, j // njg).astype(BF16)

        acc = jnp.dot(xs[...], w_r[...], preferred_element_type=F32)
        if epilogue is None:
            for o in out_r:
                o[...] = acc.astype(o.dtype)
            return
        acc_s[...] = acc

        def chunk(c, carry):
            rs = pl.ds(pl.multiple_of(c * rc, rc), rc)
            res = epilogue(acc_s[rs, :], [r[rs, :] for r in erow_r],
                           [r[rs, :] if r.shape[0] == tm and tm > 1 else r[...] for r in ebc_r])
            for o, r in zip(out_r, res):
                o[rs, :] = r.astype(o.dtype)
            return carry

        lax.fori_loop(0, tm // rc, chunk, 0)

    in_specs = ([pl.BlockSpec((tm, k), lambda i, j, cb=cb: (i, cb)) for cb in prow_blk]
                + [pl.BlockSpec(a.shape, lambda i, j: (0, 0)) for a in pbc]
                + [pl.BlockSpec((k, tn), lambda i, j: (0, j))]
                + [pl.BlockSpec((tm, tn), lambda i, j: (i, j)) for _ in erow]
                + [pl.BlockSpec((a.shape[0], tn), lambda i, j: (0, j)) for a in ebc])
    outs = pl.pallas_call(
        body,
        grid=(m // tm, nj),
        in_specs=in_specs,
        out_specs=[pl.BlockSpec((tm, tn), lambda i, j: (i, j)) for _ in out_dtypes],
        out_shape=[jax.ShapeDtypeStruct((m, n), dt) for dt in out_dtypes],
        scratch_shapes=([pltpu.VMEM((tm, k), BF16)] if prologue is not None else [])
                       + ([pltpu.VMEM((tm, tn), F32)] if epilogue is not None else []),
        compiler_params=_params(("arbitrary", "arbitrary")),
        name=name,
    )(*prow, *pbc, w, *erow, *ebc)
    return outs


def _residual_epilogue(acc, erows, ebcs):
    return (erows[0] + ebcs[0] * acc,)


def _group_rms_epilogue(with_bf16):
    def epilogue(acc, erows, ebcs):
        m0 = _half_mask()
        gain = ebcs[0]
        cols = []
        for s in range(acc.shape[1] // LANES):
            x = acc[:, s * LANES:(s + 1) * LANES]
            ms = _group64_sum(x * x, m0) * (1.0 / 64.0)
            cols.append(x * lax.rsqrt(ms + RMS_EPS) * gain[:, s * LANES:(s + 1) * LANES])
        y = jnp.concatenate(cols, axis=1) if len(cols) > 1 else cols[0]
        return (y, y) if with_bf16 else (y,)
    return epilogue


def _mix_prologue(prow_r, pbc_r, g):
    h = prow_r[0][...]
    mu = pbc_r[0][pl.ds(g, 1), :]
    return h + (prow_r[1][...] - h) * mu


def _lora_body(h_ref, hp_ref, mu_ref, w0_ref, w1_ref, w2_ref, a0_ref, a1_ref, a2_ref, lw_ref, cum_ref, a_ref,
               *, seg):
    h = h_ref[...]
    dx = hp_ref[...] - h
    xw = h + dx * mu_ref[4:5, :]
    xa = h + dx * mu_ref[5:6, :]
    wl = w0_ref[...] + _dot(jnp.tanh(_dot(xw, w1_ref[...])), w2_ref[...])
    z = -wl
    softplus = jnp.maximum(z, 0.0) + jnp.log(1.0 + jnp.exp(-jnp.abs(z)))
    lw = -jnp.exp(-softplus - 0.5)
    lw_ref[...] = lw
    tm = lw.shape[0]
    row = lax.broadcasted_iota(jnp.int32, (tm, tm), 0)
    col = lax.broadcasted_iota(jnp.int32, (tm, tm), 1)
    seg_shift = int(math.log2(seg))
    tril = ((row >= col) & ((row >> seg_shift) == (col >> seg_shift))).astype(BF16)
    hi = lw.astype(BF16)
    r1 = lw - hi.astype(F32)
    mid = r1.astype(BF16)
    lo = (r1 - mid.astype(F32)).astype(BF16)
    cum_ref[...] = (jnp.dot(tril, hi, preferred_element_type=F32)
                    + jnp.dot(tril, mid, preferred_element_type=F32)
                    + jnp.dot(tril, lo, preferred_element_type=F32))
    al = a0_ref[...] + _dot(_dot(xa, a1_ref[...]), a2_ref[...])
    a_ref[...] = 1.0 / (1.0 + jnp.exp(-al))


def _lora(h, hprev, mu, w0, w1, w2, a0, a1, a2, tm, seg):
    m, d = h.shape
    assert tm % seg == 0 and seg & (seg - 1) == 0
    row = pl.BlockSpec((tm, d), lambda i: (i, 0))
    full = lambda a: pl.BlockSpec(a.shape, lambda i: (0, 0))
    args = (h, hprev, mu, w0.reshape(1, d), w1, w2, a0.reshape(1, d), a1, a2)
    return pl.pallas_call(
        functools.partial(_lora_body, seg=seg),
        grid=(m // tm,),
        in_specs=[row, row] + [full(a) for a in args[2:]],
        out_specs=[row, row, row],
        out_shape=[jax.ShapeDtypeStruct((m, d), F32)] * 3,
        compiler_params=_params(("arbitrary",)),
        name="rwkv_lora",
    )(*args)


def _wkv_body(r_ref, k_ref, v_ref, g_ref, cum_ref, lw_ref, a_ref, kk_ref, ka_ref, rk_ref, lnw_ref, lnb_ref, s0_ref,
              o_ref, sf_ref, z_ref, *, chunk, n_chunks, pairs):
    c2 = 2 * chunk
    t_blk = pl.program_id(2)
    m0 = _half_mask()

    @pl.when(t_blk == 0)
    def _():
        zeros = jnp.zeros((RWKV_HEAD, RWKV_HEAD), F32)
        for pi in range(pairs):
            top = jnp.concatenate([s0_ref[0, 2 * pi], zeros], axis=1)
            bot = jnp.concatenate([zeros, s0_ref[0, 2 * pi + 1]], axis=1)
            z_ref[pi] = jnp.concatenate([top, bot], axis=0)

    row = lax.broadcasted_iota(jnp.int32, (c2, c2), 0)
    col = lax.broadcasted_iota(jnp.int32, (c2, c2), 1)
    strict = row > col
    eye = (row == col).astype(F32)
    incl2 = (lax.broadcasted_iota(jnp.int32, (c2, 2 * c2), 0)
             >= (lax.broadcasted_iota(jnp.int32, (c2, 2 * c2), 1) & (c2 - 1)))
    n_double = int(math.log2(chunk)) - 1

    def stack(x):
        return jnp.concatenate([jnp.where(m0, x, 0.0), jnp.where(m0, 0.0, x)], axis=0)

    prs = range(pairs)
    lanes = [slice(p * LANES, (p + 1) * LANES) for p in prs]

    def step(c, carry):
        sl = pl.ds(pl.multiple_of(c * chunk, chunk), chunk)
        r = [r_ref[sl, l] for l in lanes]
        kr = [k_ref[sl, l] for l in lanes]
        v = [v_ref[sl, l] for l in lanes]
        cum = [cum_ref[sl, l] for l in lanes]
        lw = [lw_ref[sl, l] for l in lanes]
        asg = [a_ref[sl, l] for l in lanes]

        kk = [kr[p] * kk_ref[:, lanes[p]] for p in prs]
        kk = [x / jnp.maximum(jnp.sqrt(_group64_sum(x * x, m0)), 1e-12) for x in kk]
        k = [kr[p] * (1.0 + (asg[p] - 1.0) * ka_ref[:, lanes[p]]) for p in prs]

        cum_end = [x[chunk - 1:chunk, :] for x in cum]
        e_out = [jnp.exp(cum_end[p] - cum[p]) for p in prs]
        a_hat = [-kk[p] * jnp.exp(cum[p] - lw[p] - cum_end[p]) for p in prs]
        r_hat = [r[p] * jnp.exp(cum[p] - cum_end[p]) for p in prs]
        b_hat = [kk[p] * asg[p] * e_out[p] for p in prs]
        k_hat = [k[p] * e_out[p] for p in prs]

        ar = [jnp.concatenate([stack(a_hat[p]), stack(r_hat[p])], axis=0) for p in prs]
        bk = [jnp.concatenate([stack(b_hat[p]), stack(k_hat[p])], axis=0) for p in prs]
        v_st = [stack(x) for x in v]

        mm = [_dot_nt(ar[p], bk[p]) for p in prs]
        l_ab = [jnp.where(strict, x[:c2, :c2], 0.0) for x in mm]
        l_ak = [jnp.where(strict, x[:c2, c2:], 0.0) for x in mm]
        l_r = [jnp.where(incl2, x[c2:, :], 0.0) for x in mm]

        t_inv = [eye + x for x in l_ab]
        l_pow = l_ab
        for _ in range(n_double):
            l_pow = [_dot(x, x) for x in l_pow]
            t_inv = [t + _dot(x, t) for x, t in zip(l_pow, t_inv)]

        lv = [_dot(l_ak[p], v_st[p]) for p in prs]
        zd = [z_ref[p] * jnp.exp(cum_end[p]) for p in prs]
        pz = [_dot_nt(ar[p], zd[p]) for p in prs]
        u = [_dot(t_inv[p], pz[p][:c2] + lv[p]) for p in prs]
        uv = [jnp.concatenate([u[p], v_st[p]], axis=0) for p in prs]
        y_st = [pz[p][c2:] + _dot(l_r[p], uv[p]) for p in prs]
        for p in prs:
            z_ref[p] = zd[p] + _dot_tn(uv[p], bk[p])

        for p in prs:
            y = y_st[p][:chunk] + y_st[p][chunk:]
            mean = _group64_sum(y, m0) * (1.0 / 64.0)
            yc = y - mean
            var = _group64_sum(yc * yc, m0) * (1.0 / 64.0)
            bonus = _group64_sum(r[p] * k[p] * rk_ref[:, lanes[p]], m0) * v[p]
            o = yc * lax.rsqrt(var + LNX_EPS) * lnw_ref[:, lanes[p]] + lnb_ref[:, lanes[p]] + bonus
            o_ref[sl, lanes[p]] = (o * _silu(g_ref[sl, lanes[p]])).astype(o_ref.dtype)
        return carry

    lax.fori_loop(0, n_chunks, step, 0)

    @pl.when(t_blk == pl.num_programs(2) - 1)
    def _():
        for pi in range(pairs):
            z = z_ref[pi]
            sf_ref[0, 2 * pi] = z[:RWKV_HEAD, :RWKV_HEAD]
            sf_ref[0, 2 * pi + 1] = z[RWKV_HEAD:, RWKV_HEAD:]


def _wkv(rkvg, cum, lw, asig, k_k, k_a, r_k, lnx_w, lnx_b, s0, *, batch, t_len, t_blk):
    m, d = lw.shape
    pairs = next(p for p in (WKV_PAIRS, 8, 4, 2, 1) if (d // LANES) % p == 0)
    width = pairs * LANES
    hp = d // width
    nt = t_len // t_blk
    heads = d // RWKV_HEAD
    vec = lambda a: a.reshape(1, d)
    row = lambda off: pl.BlockSpec((t_blk, width), lambda b, p, t, off=off: (b * nt + t, off + p))
    par = pl.BlockSpec((1, width), lambda b, p, t: (0, p))
    st = pl.BlockSpec((1, 2 * pairs, RWKV_HEAD, RWKV_HEAD), lambda b, p, t: (b, p, 0, 0))
    body = functools.partial(_wkv_body, chunk=WKV_CHUNK, n_chunks=t_blk // WKV_CHUNK, pairs=pairs)
    return pl.pallas_call(
        body,
        grid=(batch, hp, nt),
        in_specs=[row(0), row(hp), row(2 * hp), row(3 * hp), row(0), row(0), row(0),
                  par, par, par, par, par, st],
        out_specs=[row(0), st],
        out_shape=[jax.ShapeDtypeStruct((m, d), BF16),
                   jax.ShapeDtypeStruct((batch, heads, RWKV_HEAD, RWKV_HEAD), F32)],
        scratch_shapes=[pltpu.VMEM((pairs, LANES, LANES), F32)],
        compiler_params=_params(("arbitrary", "arbitrary", "arbitrary")),
        name="wkv7_chunked",
    )(rkvg, rkvg, rkvg, rkvg, cum, lw, asig, vec(k_k), vec(k_a), vec(r_k), vec(lnx_w), vec(lnx_b), s0)


def _lambda_val(lam_ref, lam_init):
    lq = lam_ref[...]
    s1 = jnp.sum(lq[0:1] * lq[1:2], axis=-1, keepdims=True)
    s2 = jnp.sum(lq[2:3] * lq[3:4], axis=-1, keepdims=True)
    return jnp.exp(s1) - jnp.exp(s2) + lam_init


def _sub_ln(o, g, lam_init):
    return o * lax.rsqrt(jnp.mean(o * o, axis=-1, keepdims=True) + RMS_EPS) * g * (1.0 - lam_init)


def _head_slope(h, heads):
    return jnp.exp2((-8.0 / heads) * (h + 1).astype(F32))


def _online_softmax_step(s, v_aug, m_ref, l_ref, acc_ref):
    tiles = [s[:, c * LANES:(c + 1) * LANES] for c in range(s.shape[1] // LANES)]
    mt = functools.reduce(jnp.maximum, tiles)
    m_prev = m_ref[...]
    m_next = jnp.maximum(m_prev, jnp.max(mt, axis=-1, keepdims=True))
    p = [jnp.exp2(t - m_next).astype(BF16) for t in tiles]
    p = jnp.concatenate(p, axis=1) if len(p) > 1 else p[0]
    alpha = jnp.exp2(m_prev - m_next)
    pv = jnp.dot(p, v_aug, preferred_element_type=F32)
    acc_ref[...] = alpha * acc_ref[...] + pv[:, :LANES]
    l_ref[...] = alpha * l_ref[...] + pv[:, LANES:]
    m_ref[...] = m_next


def _attn_body(q_ref, k_ref, v_ref, gz_ref, lam_ref, g_ref, o_ref, q2_ref, va_ref, m_ref, l_ref, acc_ref, *,
               tq, wide, heads, lam_init):
    h = pl.program_id(0)
    i = pl.program_id(1)
    m0 = _half_mask()
    slope = _head_slope(jnp.full((1, 1), h, jnp.int32), heads) * LOG2E

    @pl.when(i == 0)
    def _():
        va_ref[:, :LANES] = v_ref[...]
        va_ref[:, LANES:] = jnp.ones(v_ref.shape, BF16)

    q = q_ref[...].astype(F32) * (ATTN_SCALE * LOG2E)
    q2_ref[...] = jnp.concatenate([jnp.where(m0, q, 0.0), jnp.where(m0, 0.0, q)], axis=0).astype(BF16)
    m_ref[...] = jnp.full(m_ref.shape, NEG_INF, F32)
    l_ref[...] = jnp.zeros(l_ref.shape, F32)
    acc_ref[...] = jnp.zeros(acc_ref.shape, F32)

    q_start = i * tq

    def block(k_start, n_sub, diagonal_last):
        sls = [pl.ds(pl.multiple_of(k_start + c * tq, tq), tq) for c in range(n_sub)]
        col = lax.broadcasted_iota(jnp.int32, (1, tq), 1)
        ss = [lax.dot_general(q2_ref[...], k_ref[sl, :], _NT, preferred_element_type=F32) for sl in sls]
        for c in range(n_sub):
            s = ss[c] + slope * (k_start + c * tq - q_start + col).astype(F32)
            if diagonal_last and c == n_sub - 1:
                rr = lax.broadcasted_iota(jnp.int32, (2 * tq, tq), 0) & (tq - 1)
                cc = lax.broadcasted_iota(jnp.int32, (2 * tq, tq), 1)
                s = jnp.where(cc <= rr, s, NEG_INF)
            _online_softmax_step(s, va_ref[sls[c], :], m_ref, l_ref, acc_ref)

    sub_per_wide = wide // tq
    n_wide = q_start // wide
    n_narrow = (q_start - n_wide * wide) // tq

    def wide_block(j, carry):
        block(j * wide, sub_per_wide, False)
        return carry

    lax.fori_loop(0, n_wide, wide_block, 0)
    for v in range(sub_per_wide):
        @pl.when(n_narrow == v)
        def _():
            block(n_wide * wide, v + 1, True)

    lam = _lambda_val(lam_ref, lam_init)
    inv_l = 1.0 / l_ref[...]
    acc = acc_ref[...] * inv_l
    o = acc[:tq] - lam * acc[tq:]
    o_ref[...] = (_sub_ln(o, g_ref[...], lam_init) * _silu(gz_ref[...])).astype(o_ref.dtype)


def _attn_prompt(q, k, v, gz, lam, subln, lam_init, tq):
    t, hd = q.shape
    heads = hd // LANES
    wide = min(4 * tq, t)
    body = functools.partial(_attn_body, tq=tq, wide=wide, heads=heads, lam_init=lam_init)
    return pl.pallas_call(
        body,
        grid=(heads, t // tq),
        in_specs=[pl.BlockSpec((tq, LANES), lambda h, i: (i, h)),
                  pl.BlockSpec((t, LANES), lambda h, i: (0, h)),
                  pl.BlockSpec((t, LANES), lambda h, i: (0, h)),
                  pl.BlockSpec((tq, LANES), lambda h, i: (i, h)),
                  pl.BlockSpec(lam.shape, lambda h, i: (0, 0)),
                  pl.BlockSpec((1, LANES), lambda h, i: (0, 0))],
        out_specs=pl.BlockSpec((tq, LANES), lambda h, i: (i, h)),
        out_shape=jax.ShapeDtypeStruct((t, hd), BF16),
        scratch_shapes=[pltpu.VMEM((2 * tq, LANES), BF16),
                        pltpu.VMEM((t, 2 * LANES), BF16),
                        pltpu.VMEM((2 * tq, LANES), F32),
                        pltpu.VMEM((2 * tq, LANES), F32),
                        pltpu.VMEM((2 * tq, LANES), F32)],
        compiler_params=_params(("arbitrary", "arbitrary")),
        name="diff_attn_prompt",
    )(q, k, v, gz, lam, subln.reshape(1, LANES))


TSLOT = 8
PAGES_PER_STEP = 8
PAGE_GROUPS = 4


def _paged_body(pt_ref, q_ref, *refs, heads, n_pages, t_new, lam_init):
    ck_refs = refs[:PAGES_PER_STEP]
    cv_refs = refs[PAGES_PER_STEP:2 * PAGES_PER_STEP]
    (kn_ref, vn_ref, gz_ref, lam_ref, g_ref, o_ref,
     q2_ref, b0_ref, m_ref, l_ref, acc_ref) = refs[2 * PAGES_PER_STEP:]
    j = pl.program_id(1)
    n_steps = n_pages // PAGES_PER_STEP
    ng = min(PAGE_GROUPS, heads)
    hpg = heads // ng
    hrows = 2 * TSLOT
    grows = hpg * hrows
    rows = heads * hrows
    m0 = _half_mask()
    sh = lambda n: int(math.log2(n))

    ridx = lax.broadcasted_iota(jnp.int32, (rows, 1), 0)
    row_hh = (ridx >> sh(hrows)) & (hpg - 1)
    slope = _head_slope((ridx >> sh(grows)) + ng * row_hh, heads)
    tpos = ridx & (TSLOT - 1)

    def key_layout(width):
        col = lax.broadcasted_iota(jnp.int32, (1, width), 1)
        return col >> sh(hpg), col & (hpg - 1)

    @pl.when((pl.program_id(0) == 0) & (j == 0))
    def _():
        width = PAGES_PER_STEP * PAGE * hpg
        tok, col_hh = key_layout(width)
        dist = (PAGES_PER_STEP * PAGE - tok + tpos).astype(F32)
        b0_ref[...] = jnp.where(col_hh == row_hh, -slope * dist, NEG_INF)

    @pl.when(j == 0)
    def _():
        q = q_ref[0].astype(F32) * ATTN_SCALE
        for h in range(heads):
            qh = q[:, h * LANES:(h + 1) * LANES]
            base = (h % ng) * grows + (h // ng) * hrows
            q2_ref[base:base + hrows, :] = jnp.concatenate(
                [jnp.where(m0, qh, 0.0), jnp.where(m0, 0.0, qh)], axis=0).astype(BF16)
        m_ref[...] = jnp.full(m_ref.shape, NEG_INF, F32)
        l_ref[...] = jnp.zeros(l_ref.shape, F32)
        acc_ref[...] = jnp.zeros(acc_ref.shape, F32)

    def attend(k_tiles, v_tiles, bias):
        def gather(tiles, g):
            parts = [r[0, pl.ds(t * LANES * ng + g, LANES, stride=ng), :] for r, t in tiles]
            x = jnp.concatenate(parts, axis=0) if len(parts) > 1 else parts[0]
            return x.astype(BF16)

        s = jnp.concatenate(
            [lax.dot_general(q2_ref[g * grows:(g + 1) * grows, :], gather(k_tiles, g), _NT,
                             preferred_element_type=F32) for g in range(ng)], axis=0) + bias
        tiles = [s[:, c * LANES:(c + 1) * LANES] for c in range(s.shape[1] // LANES)]
        m_prev = m_ref[...]
        m_next = jnp.maximum(m_prev, jnp.max(functools.reduce(jnp.maximum, tiles), axis=-1, keepdims=True))
        p = [jnp.exp(t - m_next) for t in tiles]
        alpha = jnp.exp(m_prev - m_next)
        l_ref[...] = alpha * l_ref[...] + jnp.sum(functools.reduce(jnp.add, p), axis=-1, keepdims=True)
        p = [t.astype(BF16) for t in p]
        p = jnp.concatenate(p, axis=1) if len(p) > 1 else p[0]
        pv = jnp.concatenate(
            [jnp.dot(p[g * grows:(g + 1) * grows, :], gather(v_tiles, g), preferred_element_type=F32)
             for g in range(ng)], axis=0)
        acc_ref[...] = alpha * acc_ref[...] + pv
        m_ref[...] = m_next

    @pl.when(j < n_steps)
    def _():
        gap = ((n_steps - 1 - j) * (PAGES_PER_STEP * PAGE)).astype(F32)
        attend([(r, t) for r in ck_refs for t in range(hpg)],
               [(r, t) for r in cv_refs for t in range(hpg)],
               b0_ref[...] - slope * gap)

    @pl.when(j == n_steps)
    def _():
        tok, col_hh = key_layout(LANES)
        valid = (col_hh == row_hh) & (tok <= tpos) & (tok < t_new)
        bias = jnp.where(valid, -slope * (tpos - tok).astype(F32), NEG_INF)
        attend([(kn_ref, 0)], [(vn_ref, 0)], bias)
        lam = _lambda_val(lam_ref, lam_init)
        acc = acc_ref[...] * (1.0 / l_ref[...])
        for h in range(heads):
            base = (h % ng) * grows + (h // ng) * hrows
            o = acc[base:base + TSLOT] - lam * acc[base + TSLOT:base + 2 * TSLOT]
            hl = slice(h * LANES, (h + 1) * LANES)
            o_ref[0, :, hl] = _sub_ln(o, g_ref[...], lam_init) * _silu(gz_ref[0, :, hl])


def _attn_paged(q, k_new, v_new, gz, cache_k, cache_v, page_table, lam, subln, lam_init, t_new):
    bsz, _, hd = q.shape
    heads = hd // LANES
    n_pages = page_table.shape[1]
    assert n_pages % PAGES_PER_STEP == 0
    n_steps = n_pages // PAGES_PER_STEP
    rows = heads * 2 * TSLOT
    ng = min(PAGE_GROUPS, heads)
    hpg = heads // ng
    assert heads == ng * hpg and hpg & (hpg - 1) == 0 and t_new <= LANES // hpg and t_new <= TSLOT
    body = functools.partial(_paged_body, heads=heads, n_pages=n_pages, t_new=t_new, lam_init=lam_init)

    def page(a):
        idx = lambda b, j, pt: (pt[b, jnp.minimum(j, n_steps - 1) * PAGES_PER_STEP + a], 0, 0)
        return pl.BlockSpec((1, PAGE * heads, LANES), idx)

    per_b = lambda b, j, pt: (b, 0, 0)
    grid_spec = pltpu.PrefetchScalarGridSpec(
        num_scalar_prefetch=1,
        grid=(bsz, n_steps + 1),
        in_specs=[pl.BlockSpec((1, TSLOT, hd), per_b),
                  *[page(a) for a in range(PAGES_PER_STEP)], *[page(a) for a in range(PAGES_PER_STEP)],
                  pl.BlockSpec((1, LANES * ng, LANES), per_b),
                  pl.BlockSpec((1, LANES * ng, LANES), per_b),
                  pl.BlockSpec((1, TSLOT, hd), per_b),
                  pl.BlockSpec(lam.shape, lambda b, j, pt: (0, 0)),
                  pl.BlockSpec((1, LANES), lambda b, j, pt: (0, 0))],
        out_specs=pl.BlockSpec((1, TSLOT, hd), per_b),
        scratch_shapes=[pltpu.VMEM((rows, LANES), BF16),
                        pltpu.VMEM((rows, PAGES_PER_STEP * PAGE * hpg), F32),
                        pltpu.VMEM((rows, LANES), F32),
                        pltpu.VMEM((rows, LANES), F32),
                        pltpu.VMEM((rows, LANES), F32)])
    return pl.pallas_call(
        body,
        grid_spec=grid_spec,
        out_shape=jax.ShapeDtypeStruct((bsz, TSLOT, hd), F32),
        compiler_params=_params(("arbitrary", "arbitrary")),
        name="diff_attn_paged",
    )(page_table, q, *([cache_k] * PAGES_PER_STEP), *([cache_v] * PAGES_PER_STEP), k_new, v_new, gz, lam,
      subln.reshape(1, LANES))


def _row_tile(m):
    return min(1024, m)


def _rwkv_layer(x, shift, scale, gate, h_prev_row, s0, p, *, batch, t_len):
    m, d = x.shape
    tm = _row_tile(m)
    tn = _col_tile(d)
    h = _modnorm(x, p["norm"], shift, scale, tm)
    h3 = h.reshape(batch, t_len, d)
    hprev = jnp.concatenate([h_prev_row[:, None, :], h3[:, :-1]], axis=1).reshape(m, d)

    rkvg, = _fused_linear("rwkv_in_proj", _mix_prologue, None, [h, hprev], [p["mu"]],
                          p["w_in"], [], [], [F32], tm=tm, tn=tn, groups=4)
    seg = WKV_CHUNK if t_len >= WKV_CHUNK else t_len
    assert t_len % seg == 0
    lw, cum, asig = _lora(h, hprev, p["mu"], p["w0"], p["w1"], p["w2"], p["a0"], p["a1"], p["a2"],
                          min(256, m), seg)

    t_pad = -(-t_len // WKV_CHUNK) * WKV_CHUNK
    if t_pad != t_len:
        def padr(a, mode="constant"):
            return jnp.pad(a.reshape(batch, t_len, -1), ((0, 0), (0, t_pad - t_len), (0, 0)),
                           mode=mode).reshape(batch * t_pad, -1)
        rkvg_w, cum_w, lw_w, asig_w = padr(rkvg), padr(cum, "edge"), padr(lw), padr(asig)
    else:
        rkvg_w, cum_w, lw_w, asig_w = rkvg, cum, lw, asig
    t_blk = min(256, t_pad)
    o_pre, s_fin = _wkv(rkvg_w, cum_w, lw_w, asig_w, p["k_k"], p["k_a"], p["r_k"], p["lnx_w"], p["lnx_b"],
                        s0, batch=batch, t_len=t_pad, t_blk=t_blk)
    if t_pad != t_len:
        o_pre = o_pre.reshape(batch, t_pad, d)[:, :t_len].reshape(m, d)

    x_new, = _fused_linear("rwkv_out_proj", None, _residual_epilogue, [o_pre], [],
                           p["w_out"], [x], [gate], [F32], tm=tm, tn=_wide_col_tile(d))
    return x_new, s_fin, h3[:, -1]


def _tile_gain(gain, n):
    return jnp.tile(gain.reshape(1, LANES), (1, n // LANES))


def _shared_kv(x, shift, scale, p):
    m, d = x.shape
    tm = _row_tile(m)
    qd = p["w_k"].shape[1]
    xn = _modnorm(x, p["norm"], shift, scale, tm, BF16)
    k, k16 = _fused_linear("kv_k_proj", None, _group_rms_epilogue(True), [xn], [], p["w_k"],
                           [], [_tile_gain(p["k_norm"], qd)], [F32, BF16], tm=tm, tn=_wide_col_tile(qd))
    v, v16 = _fused_linear("kv_v_proj", None, None, [xn], [], p["w_v"],
                           [], [], [F32, BF16], tm=tm, tn=_wide_col_tile(p["w_v"].shape[1]))
    return k, v, k16, v16


def _diff_queries(x, shift, scale, p):
    m, d = x.shape
    tm = _row_tile(m)
    qd = p["w_q"].shape[1]
    xn = _modnorm(x, p["norm"], shift, scale, tm, BF16)
    q16, = _fused_linear("diff_q_proj", None, _group_rms_epilogue(False), [xn], [], p["w_q"],
                         [], [_tile_gain(p["q_norm"], qd)], [BF16], tm=tm, tn=_wide_col_tile(qd))
    gz, = _fused_linear("diff_gate_proj", None, None, [xn], [], p["w_z"],
                        [], [], [F32], tm=tm, tn=_wide_col_tile(p["w_z"].shape[1]))
    return q16, gz


def _diff_out(x, o16, gate, w_out):
    m, d = x.shape
    y, = _fused_linear("diff_out_proj", None, _residual_epilogue, [o16], [], w_out,
                       [x], [gate], [F32], tm=_row_tile(m), tn=_wide_col_tile(d))
    return y


def kernel(x_prompt, x_sample, c_prompt, c_sample, state_wkv, state_shift, cache_k, cache_v, page_table,
           a_mod_w, a_mod_b, a_norm, a_mu, a_w_in, a_w0, a_w1, a_w2, a_a0, a_a1, a_a2,
           a_k_k, a_k_a, a_r_k, a_lnx_w, a_lnx_b, a_w_out,
           kv_mod_w, kv_mod_b, kv_norm, kv_w, kv_k_norm,
           b_mod_w, b_mod_b, b_norm, b_w_in, b_q_norm, b_lam, b_subln, b_w_out):
    bp, t_p, d = x_prompt.shape
    bs, t_s, _ = x_sample.shape
    n_a = a_mod_w.shape[0]
    n_b = b_mod_w.shape[0]
    depth = n_a + n_b
    heads_r = d // RWKV_HEAD
    q_dim = b_w_in.shape[2] - d
    lora = a_w1.shape[2]
    lora_pad = -(-lora // LANES) * LANES

    xp = x_prompt.reshape(bp * t_p, d)
    xs = x_sample.reshape(bs * t_s, d)

    n_c = bp + bs
    c_all = jnp.pad(jnp.concatenate([c_prompt, c_sample], axis=0), ((0, -n_c % 16), (0, 0)))

    def mods(w, b, n):
        mv = _mod_vectors(c_all, w, b)
        parts = jnp.split(mv, n, axis=-1)
        assert bp == 1
        prompt = [q[0:1] for q in parts]
        sample = [jnp.repeat(q[bp:n_c], t_s, axis=0) for q in parts]
        return prompt, sample

    wkv_p, wkv_s, shift_p, shift_s = [], [], [], []
    for layer in range(n_a):
        prm = dict(
            norm=a_norm[layer], mu=a_mu[layer],
            w_in=a_w_in[layer].reshape(d, 4 * d).astype(BF16),
            w0=a_w0[layer], a0=a_a0[layer],
            w1=jnp.pad(a_w1[layer], ((0, 0), (0, lora_pad - lora))).astype(BF16),
            w2=jnp.pad(a_w2[layer], ((0, lora_pad - lora), (0, 0))).astype(BF16),
            a1=jnp.pad(a_a1[layer], ((0, 0), (0, lora_pad - lora))).astype(BF16),
            a2=jnp.pad(a_a2[layer], ((0, lora_pad - lora), (0, 0))).astype(BF16),
            k_k=a_k_k[layer], k_a=a_k_a[layer], r_k=a_r_k[layer],
            lnx_w=a_lnx_w[layer], lnx_b=a_lnx_b[layer],
            w_out=a_w_out[layer].astype(BF16))
        (sh_p, sc_p, gt_p), (sh_s, sc_s, gt_s) = mods(a_mod_w[layer], a_mod_b[layer], 3)
        xp, s_fin, last = _rwkv_layer(
            xp, sh_p, sc_p, gt_p, jnp.zeros((bp, d), F32),
            jnp.zeros((bp, heads_r, RWKV_HEAD, RWKV_HEAD), F32), prm, batch=bp, t_len=t_p)
        wkv_p.append(s_fin)
        shift_p.append(last)
        xs, s_fin, last = _rwkv_layer(
            xs, sh_s, sc_s, gt_s, state_shift[layer], state_wkv[layer], prm, batch=bs, t_len=t_s)
        wkv_s.append(s_fin)
        shift_s.append(last)

    (sh_p, sc_p), (sh_s, sc_s) = mods(kv_mod_w, kv_mod_b, 2)
    kvp = dict(norm=kv_norm, w_k=kv_w[:, :q_dim].astype(BF16), w_v=kv_w[:, q_dim:].astype(BF16),
               k_norm=kv_k_norm)
    k_p, v_p, k_p16, v_p16 = _shared_kv(xp, sh_p, sc_p, kvp)
    k_s, v_s, _, _ = _shared_kv(xs, sh_s, sc_s, kvp)

    n_pool = cache_k.shape[0]
    pad_tok = lambda a, n: jnp.pad(a.reshape(bs, t_s, -1), ((0, 0), (0, n - t_s), (0, 0)))
    heads_q = q_dim // LANES
    new_tok = LANES * min(PAGE_GROUPS, heads_q) // heads_q
    as_page = lambda a: pad_tok(a, new_tok).reshape(bs, new_tok * heads_q, LANES)

    for i in range(n_b):
        layer = n_a + i
        lam_init = 0.8 - 0.6 * math.exp(-0.3 * layer)
        qp = dict(norm=b_norm[i], w_q=b_w_in[i][:, :q_dim].astype(BF16),
                  w_z=b_w_in[i][:, q_dim:].astype(BF16), q_norm=b_q_norm[i])
        (sh_p, sc_p, gt_p), (sh_s, sc_s, gt_s) = mods(b_mod_w[i], b_mod_b[i], 3)
        w_out = b_w_out[i].astype(BF16)

        q16, gz = _diff_queries(xp, sh_p, sc_p, qp)
        o16 = _attn_prompt(q16, k_p16, v_p16, gz, b_lam[i], b_subln[i], lam_init, tq=min(512, t_p))
        xp = _diff_out(xp, o16, gt_p, w_out)

        q16, gz = _diff_queries(xs, sh_s, sc_s, qp)
        o = _attn_paged(pad_tok(q16.astype(F32), TSLOT), as_page(k_s), as_page(v_s), pad_tok(gz, TSLOT),
                        cache_k.reshape(n_pool, -1, LANES), cache_v.reshape(n_pool, -1, LANES), page_table,
                        b_lam[i], b_subln[i], lam_init, t_s)
        xs = _diff_out(xs, o[:, :t_s].reshape(bs * t_s, -1).astype(BF16), gt_s, w_out)

    heads_d = q_dim // LANES
    return (xp.reshape(bp, t_p, d), xs.reshape(bs, t_s, d),
            jnp.stack(wkv_p), jnp.stack(shift_p),
            k_p.reshape(bp, t_p, heads_d, LANES), v_p.reshape(bp, t_p, heads_d, LANES),
            jnp.stack(wkv_s), jnp.stack(shift_s),
            k_s.reshape(bs, t_s, heads_d, LANES), v_s.reshape(bs, t_s, heads_d, LANES))
```

```python
import functools
import math

import jax
import jax.numpy as jnp
from jax import lax
from jax.experimental import pallas as pl
from jax.experimental.pallas import tpu as pltpu

F32 = jnp.float32
BF16 = jnp.bfloat16

LANES = 128
RWKV_HEAD = 64
DIFF_HEAD = 64
PAGE = 128
RMS_EPS = 1e-6
LNX_EPS = 64e-5
NEG_INF = -1e30
ATTN_SCALE = DIFF_HEAD ** -0.5
LOG2E = 1.4426950408889634
EPILOGUE_ROWS = 256
WKV_CHUNK = 64
WKV_PAIRS = 16
VMEM_LIMIT = 56 * 1024 * 1024

_NT = (((1,), (1,)), ((), ()))
_TN = (((0,), (0,)), ((), ()))


def _dot(a, b):
    return jnp.dot(a.astype(BF16), b.astype(BF16), preferred_element_type=F32)


def _dot_nt(a, b):
    return lax.dot_general(a.astype(BF16), b.astype(BF16), _NT, preferred_element_type=F32)


def _dot_tn(a, b):
    return lax.dot_general(a.astype(BF16), b.astype(BF16), _TN, preferred_element_type=F32)


def _silu(x):
    return x * (1.0 / (1.0 + jnp.exp(-x)))


def _col_tile(n):
    return next(t for t in (512, 384, 256, 128) if n % t == 0)


def _wide_col_tile(n):
    return next(t for t in (1024, 512, 384, 256, 128) if n % t == 0)


def _params(sem):
    return pltpu.CompilerParams(dimension_semantics=sem, vmem_limit_bytes=VMEM_LIMIT)


def _half_mask():
    return lax.broadcasted_iota(jnp.int32, (1, LANES), 1) < 64


def _group64_sum(x, m0):
    s0 = jnp.sum(jnp.where(m0, x, 0.0), axis=-1, keepdims=True)
    s1 = jnp.sum(jnp.where(m0, 0.0, x), axis=-1, keepdims=True)
    return jnp.where(m0, s0, s1)


def _mod_body(c_ref, w_ref, b_ref, o_ref):
    o_ref[...] = _dot(_silu(c_ref[...]), w_ref[...]) + b_ref[...]


def _mod_vectors(c, w, b):
    m, d = c.shape
    n = w.shape[1]
    tn = _col_tile(n)
    return pl.pallas_call(
        _mod_body,
        grid=(n // tn,),
        in_specs=[pl.BlockSpec((m, d), lambda j: (0, 0)),
                  pl.BlockSpec((d, tn), lambda j: (0, j)),
                  pl.BlockSpec((1, tn), lambda j: (0, j))],
        out_specs=pl.BlockSpec((m, tn), lambda j: (0, j)),
        out_shape=jax.ShapeDtypeStruct((m, n), F32),
        compiler_params=_params(("arbitrary",)),
        name="mod_vectors",
    )(c, w, b.reshape(1, n))


def _modnorm_val(x, g, shift, scale):
    r = lax.rsqrt(jnp.mean(x * x, axis=-1, keepdims=True) + RMS_EPS)
    return (x * r) * g * (1.0 + scale) + shift


def _modnorm_body(x_ref, g_ref, sh_ref, sc_ref, o_ref):
    o_ref[...] = _modnorm_val(x_ref[...], g_ref[...], sh_ref[...], sc_ref[...]).astype(o_ref.dtype)


def _modnorm(x, g, shift, scale, tm, out_dtype=F32):
    m, d = x.shape
    rb = shift.shape[0]
    return pl.pallas_call(
        _modnorm_body,
        grid=(m // tm,),
        in_specs=[pl.BlockSpec((tm, d), lambda i: (i, 0)),
                  pl.BlockSpec((1, d), lambda i: (0, 0)),
                  pl.BlockSpec((rb, d), lambda i: (0, 0)),
                  pl.BlockSpec((rb, d), lambda i: (0, 0))],
        out_specs=pl.BlockSpec((tm, d), lambda i: (i, 0)),
        out_shape=jax.ShapeDtypeStruct((m, d), out_dtype),
        compiler_params=_params(("arbitrary",)),
        name="modnorm",
    )(x, g.reshape(1, d), shift, scale)


def _fused_linear(name, prologue, epilogue, prow, pbc, w, erow, ebc, out_dtypes, *, tm, tn, groups=1):
    prow = [a if isinstance(a, tuple) else (a, 0) for a in prow]
    prow_blk = [cb for _, cb in prow]
    prow = [a for a, _ in prow]
    m = prow[0].shape[0]
    k, n = w.shape
    nj = n // tn
    njg = nj // groups
    rc = min(EPILOGUE_ROWS, tm)
    n_prow, n_pbc, n_erow, n_ebc, n_out = len(prow), len(pbc), len(erow), len(ebc), len(out_dtypes)

    def body(*refs):
        it = iter(refs)
        prow_r = [next(it) for _ in range(n_prow)]
        pbc_r = [next(it) for _ in range(n_pbc)]
        w_r = next(it)
        erow_r = [next(it) for _ in range(n_erow)]
        ebc_r = [next(it) for _ in range(n_ebc)]
        out_r = [next(it) for _ in range(n_out)]
        xs = next(it) if prologue is not None else prow_r[0]
        acc_s = next(it) if epilogue is not None else None
        j = pl.program_id(1)

        if prologue is not None:
            @pl.when(j % njg == 0)
            def _():
                xs[...] = prologue(prow_r, pbc_r, j // njg).astype(BF16)

        acc = jnp.dot(xs[...], w_r[...], preferred_element_type=F32)
        if epilogue is None:
            for o in out_r:
                o[...] = acc.astype(o.dtype)
            return
        acc_s[...] = acc

        def chunk(c, carry):
            rs = pl.ds(pl.multiple_of(c * rc, rc), rc)
            res = epilogue(acc_s[rs, :], [r[rs, :] for r in erow_r],
                           [r[rs, :] if r.shape[0] == tm and tm > 1 else r[...] for r in ebc_r])
            for o, r in zip(out_r, res):
                o[rs, :] = r.astype(o.dtype)
            return carry

        lax.fori_loop(0, tm // rc, chunk, 0)

    in_specs = ([pl.BlockSpec((tm, k), lambda i, j, cb=cb: (i, cb)) for cb in prow_blk]
                + [pl.BlockSpec(a.shape, lambda i, j: (0, 0)) for a in pbc]
                + [pl.BlockSpec((k, tn), lambda i, j: (0, j))]
                + [pl.BlockSpec((tm, tn), lambda i, j: (i, j)) for _ in erow]
                + [pl.BlockSpec((a.shape[0], tn), lambda i, j: (0, j)) for a in ebc])
    outs = pl.pallas_call(
        body,
        grid=(m // tm, nj),
        in_specs=in_specs,
        out_specs=[pl.BlockSpec((tm, tn), lambda i, j: (i, j)) for _ in out_dtypes],
        out_shape=[jax.ShapeDtypeStruct((m, n), dt) for dt in out_dtypes],
        scratch_shapes=([pltpu.VMEM((tm, k), BF16)] if prologue is not None else [])
                       + ([pltpu.VMEM((tm, tn), F32)] if epilogue is not None else []),
        compiler_params=_params(("arbitrary", "arbitrary")),
        name=name,
    )(*prow, *pbc, w, *erow, *ebc)
    return outs


def _residual_epilogue(acc, erows, ebcs):
    return (erows[0] + ebcs[0] * acc,)


def _group_rms_epilogue(with_bf16):
    def epilogue(acc, erows, ebcs):
        m0 = _half_mask()
        gain = ebcs[0]
        cols = []
        for s in range(acc.shape[1] // LANES):
            x = acc[:, s * LANES:(s + 1) * LANES]
            ms = _group64_sum(x * x, m0) * (1.0 / 64.0)
            cols.append(x * lax.rsqrt(ms + RMS_EPS) * gain[:, s * LANES:(s + 1) * LANES])
        y = jnp.concatenate(cols, axis=1) if len(cols) > 1 else cols[0]
        return (y, y) if with_bf16 else (y,)
    return epilogue


def _mix_prologue(prow_r, pbc_r, g):
    h = prow_r[0][...]
    mu = pbc_r[0][pl.ds(g, 1), :]
    return h + (prow_r[1][...] - h) * mu


def _lora_body(h_ref, hp_ref, mu_ref, w0_ref, w1_ref, w2_ref, a0_ref, a1_ref, a2_ref, lw_ref, cum_ref, a_ref,
               *, seg):
    h = h_ref[...]
    dx = hp_ref[...] - h
    xw = h + dx * mu_ref[4:5, :]
    xa = h + dx * mu_ref[5:6, :]
    wl = w0_ref[...] + _dot(jnp.tanh(_dot(xw, w1_ref[...])), w2_ref[...])
    z = -wl
    softplus = jnp.maximum(z, 0.0) + jnp.log(1.0 + jnp.exp(-jnp.abs(z)))
    lw = -jnp.exp(-softplus - 0.5)
    lw_ref[...] = lw
    tm = lw.shape[0]
    row = lax.broadcasted_iota(jnp.int32, (tm, tm), 0)
    col = lax.broadcasted_iota(jnp.int32, (tm, tm), 1)
    seg_shift = int(math.log2(seg))
    tril = ((row >= col) & ((row >> seg_shift) == (col >> seg_shift))).astype(BF16)
    hi = lw.astype(BF16)
    r1 = lw - hi.astype(F32)
    mid = r1.astype(BF16)
    lo = (r1 - mid.astype(F32)).astype(BF16)
    cum_ref[...] = (jnp.dot(tril, hi, preferred_element_type=F32)
                    + jnp.dot(tril, mid, preferred_element_type=F32)
                    + jnp.dot(tril, lo, preferred_element_type=F32))
    al = a0_ref[...] + _dot(_dot(xa, a1_ref[...]), a2_ref[...])
    a_ref[...] = 1.0 / (1.0 + jnp.exp(-al))


def _lora(h, hprev, mu, w0, w1, w2, a0, a1, a2, tm, seg):
    m, d = h.shape
    assert tm % seg == 0 and seg & (seg - 1) == 0
    row = pl.BlockSpec((tm, d), lambda i: (i, 0))
    full = lambda a: pl.BlockSpec(a.shape, lambda i: (0, 0))
    args = (h, hprev, mu, w0.reshape(1, d), w1, w2, a0.reshape(1, d), a1, a2)
    return pl.pallas_call(
        functools.partial(_lora_body, seg=seg),
        grid=(m // tm,),
        in_specs=[row, row] + [full(a) for a in args[2:]],
        out_specs=[row, row, row],
        out_shape=[jax.ShapeDtypeStruct((m, d), F32)] * 3,
        compiler_params=_params(("arbitrary",)),
        name="rwkv_lora",
    )(*args)


def _wkv_body(r_ref, k_ref, v_ref, g_ref, cum_ref, lw_ref, a_ref, kk_ref, ka_ref, rk_ref, lnw_ref, lnb_ref, s0_ref,
              o_ref, sf_ref, z_ref, *, chunk, n_chunks, pairs):
    c2 = 2 * chunk
    t_blk = pl.program_id(2)
    m0 = _half_mask()

    @pl.when(t_blk == 0)
    def _():
        zeros = jnp.zeros((RWKV_HEAD, RWKV_HEAD), F32)
        for pi in range(pairs):
            top = jnp.concatenate([s0_ref[0, 2 * pi], zeros], axis=1)
            bot = jnp.concatenate([zeros, s0_ref[0, 2 * pi + 1]], axis=1)
            z_ref[pi] = jnp.concatenate([top, bot], axis=0)

    row = lax.broadcasted_iota(jnp.int32, (c2, c2), 0)
    col = lax.broadcasted_iota(jnp.int32, (c2, c2), 1)
    strict = row > col
    eye = (row == col).astype(F32)
    incl2 = (lax.broadcasted_iota(jnp.int32, (c2, 2 * c2), 0)
             >= (lax.broadcasted_iota(jnp.int32, (c2, 2 * c2), 1) & (c2 - 1)))
    n_double = int(math.log2(chunk)) - 1

    def stack(x):
        return jnp.concatenate([jnp.where(m0, x, 0.0), jnp.where(m0, 0.0, x)], axis=0)

    prs = range(pairs)
    lanes = [slice(p * LANES, (p + 1) * LANES) for p in prs]

    def step(c, carry):
        sl = pl.ds(pl.multiple_of(c * chunk, chunk), chunk)
        r = [r_ref[sl, l] for l in lanes]
        kr = [k_ref[sl, l] for l in lanes]
        v = [v_ref[sl, l] for l in lanes]
        cum = [cum_ref[sl, l] for l in lanes]
        lw = [lw_ref[sl, l] for l in lanes]
        asg = [a_ref[sl, l] for l in lanes]

        kk = [kr[p] * kk_ref[:, lanes[p]] for p in prs]
        kk = [x / jnp.maximum(jnp.sqrt(_group64_sum(x * x, m0)), 1e-12) for x in kk]
        k = [kr[p] * (1.0 + (asg[p] - 1.0) * ka_ref[:, lanes[p]]) for p in prs]

        cum_end = [x[chunk - 1:chunk, :] for x in cum]
        e_out = [jnp.exp(cum_end[p] - cum[p]) for p in prs]
        a_hat = [-kk[p] * jnp.exp(cum[p] - lw[p] - cum_end[p]) for p in prs]
        r_hat = [r[p] * jnp.exp(cum[p] - cum_end[p]) for p in prs]
        b_hat = [kk[p] * asg[p] * e_out[p] for p in prs]
        k_hat = [k[p] * e_out[p] for p in prs]

        ar = [jnp.concatenate([stack(a_hat[p]), stack(r_hat[p])], axis=0) for p in prs]
        bk = [jnp.concatenate([stack(b_hat[p]), stack(k_hat[p])], axis=0) for p in prs]
        v_st = [stack(x) for x in v]

        mm = [_dot_nt(ar[p], bk[p]) for p in prs]
        l_ab = [jnp.where(strict, x[:c2, :c2], 0.0) for x in mm]
        l_ak = [jnp.where(strict, x[:c2, c2:], 0.0) for x in mm]
        l_r = [jnp.where(incl2, x[c2:, :], 0.0) for x in mm]

        t_inv = [eye + x for x in l_ab]
        l_pow = l_ab
        for _ in range(n_double):
            l_pow = [_dot(x, x) for x in l_pow]
            t_inv = [t + _dot(x, t) for x, t in zip(l_pow, t_inv)]

        lv = [_dot(l_ak[p], v_st[p]) for p in prs]
        zd = [z_ref[p] * jnp.exp(cum_end[p]) for p in prs]
        pz = [_dot_nt(ar[p], zd[p]) for p in prs]
        u = [_dot(t_inv[p], pz[p][:c2] + lv[p]) for p in prs]
        uv = [jnp.concatenate([u[p], v_st[p]], axis=0) for p in prs]
        y_st = [pz[p][c2:] + _dot(l_r[p], uv[p]) for p in prs]
        for p in prs:
            z_ref[p] = zd[p] + _dot_tn(uv[p], bk[p])

        for p in prs:
            y = y_st[p][:chunk] + y_st[p][chunk:]
            mean = _group64_sum(y, m0) * (1.0 / 64.0)
            yc = y - mean
            var = _group64_sum(yc * yc, m0) * (1.0 / 64.0)
            bonus = _group64_sum(r[p] * k[p] * rk_ref[:, lanes[p]], m0) * v[p]
            o = yc * lax.rsqrt(var + LNX_EPS) * lnw_ref[:, lanes[p]] + lnb_ref[:, lanes[p]] + bonus
            o_ref[sl, lanes[p]] = (o * _silu(g_ref[sl, lanes[p]])).astype(o_ref.dtype)
        return carry

    lax.fori_loop(0, n_chunks, step, 0)

    @pl.when(t_blk == pl.num_programs(2) - 1)
    def _():
        for pi in range(pairs):
            z = z_ref[pi]
            sf_ref[0, 2 * pi] = z[:RWKV_HEAD, :RWKV_HEAD]
            sf_ref[0, 2 * pi + 1] = z[RWKV_HEAD:, RWKV_HEAD:]


def _wkv(rkvg, cum, lw, asig, k_k, k_a, r_k, lnx_w, lnx_b, s0, *, batch, t_len, t_blk):
    m, d = lw.shape
    pairs = next(p for p in (WKV_PAIRS, 8, 4, 2, 1) if (d // LANES) % p == 0)
    width = pairs * LANES
    hp = d // width
    nt = t_len // t_blk
    heads = d // RWKV_HEAD
    vec = lambda a: a.reshape(1, d)
    row = lambda off: pl.BlockSpec((t_blk, width), lambda b, p, t, off=off: (b * nt + t, off + p))
    par = pl.BlockSpec((1, width), lambda b, p, t: (0, p))
    st = pl.BlockSpec((1, 2 * pairs, RWKV_HEAD, RWKV_HEAD), lambda b, p, t: (b, p, 0, 0))
    body = functools.partial(_wkv_body, chunk=WKV_CHUNK, n_chunks=t_blk // WKV_CHUNK, pairs=pairs)
    return pl.pallas_call(
        body,
        grid=(batch, hp, nt),
        in_specs=[row(0), row(hp), row(2 * hp), row(3 * hp), row(0), row(0), row(0),
                  par, par, par, par, par, st],
        out_specs=[row(0), st],
        out_shape=[jax.ShapeDtypeStruct((m, d), BF16),
                   jax.ShapeDtypeStruct((batch, heads, RWKV_HEAD, RWKV_HEAD), F32)],
        scratch_shapes=[pltpu.VMEM((pairs, LANES, LANES), F32)],
        compiler_params=_params(("arbitrary", "arbitrary", "arbitrary")),
        name="wkv7_chunked",
    )(rkvg, rkvg, rkvg, rkvg, cum, lw, asig, vec(k_k), vec(k_a), vec(r_k), vec(lnx_w), vec(lnx_b), s0)


def _lambda_val(lam_ref, lam_init):
    lq = lam_ref[...]
    s1 = jnp.sum(lq[0:1] * lq[1:2], axis=-1, keepdims=True)
    s2 = jnp.sum(lq[2:3] * lq[3:4], axis=-1, keepdims=True)
    return jnp.exp(s1) - jnp.exp(s2) + lam_init


def _sub_ln(o, g, lam_init):
    return o * lax.rsqrt(jnp.mean(o * o, axis=-1, keepdims=True) + RMS_EPS) * g * (1.0 - lam_init)


def _head_slope(h, heads):
    return jnp.exp2((-8.0 / heads) * (h + 1).astype(F32))


def _online_softmax_step(s, v_aug, m_ref, l_ref, acc_ref):
    tiles = [s[:, c * LANES:(c + 1) * LANES] for c in range(s.shape[1] // LANES)]
    mt = functools.reduce(jnp.maximum, tiles)
    m_prev = m_ref[...]
    m_next = jnp.maximum(m_prev, jnp.max(mt, axis=-1, keepdims=True))
    p = [jnp.exp2(t - m_next).astype(BF16) for t in tiles]
    p = jnp.concatenate(p, axis=1) if len(p) > 1 else p[0]
    alpha = jnp.exp2(m_prev - m_next)
    pv = jnp.dot(p, v_aug, preferred_element_type=F32)
    acc_ref[...] = alpha * acc_ref[...] + pv[:, :LANES]
    l_ref[...] = alpha * l_ref[...] + pv[:, LANES:]
    m_ref[...] = m_next


def _attn_body(q_ref, k_ref, v_ref, gz_ref, lam_ref, g_ref, o_ref, q2_ref, va_ref, m_ref, l_ref, acc_ref, *,
               tq, wide, heads, lam_init):
    h = pl.program_id(0)
    i = pl.program_id(1)
    m0 = _half_mask()
    slope = _head_slope(jnp.full((1, 1), h, jnp.int32), heads) * LOG2E

    @pl.when(i == 0)
    def _():
        va_ref[:, :LANES] = v_ref[...]
        va_ref[:, LANES:] = jnp.ones(v_ref.shape, BF16)

    q = q_ref[...].astype(F32) * (ATTN_SCALE * LOG2E)
    q2_ref[...] = jnp.concatenate([jnp.where(m0, q, 0.0), jnp.where(m0, 0.0, q)], axis=0).astype(BF16)
    m_ref[...] = jnp.full(m_ref.shape, NEG_INF, F32)
    l_ref[...] = jnp.zeros(l_ref.shape, F32)
    acc_ref[...] = jnp.zeros(acc_ref.shape, F32)

    q_start = i * tq

    def block(k_start, n_sub, diagonal_last):
        sls = [pl.ds(pl.multiple_of(k_start + c * tq, tq), tq) for c in range(n_sub)]
        col = lax.broadcasted_iota(jnp.int32, (1, tq), 1)
        ss = [lax.dot_general(q2_ref[...], k_ref[sl, :], _NT, preferred_element_type=F32) for sl in sls]
        for c in range(n_sub):
            s = ss[c] + slope * (k_start + c * tq - q_start + col).astype(F32)
            if diagonal_last and c == n_sub - 1:
                rr = lax.broadcasted_iota(jnp.int32, (2 * tq, tq), 0) & (tq - 1)
                cc = lax.broadcasted_iota(jnp.int32, (2 * tq, tq), 1)
                s = jnp.where(cc <= rr, s, NEG_INF)
            _online_softmax_step(s, va_ref[sls[c], :], m_ref, l_ref, acc_ref)

    sub_per_wide = wide // tq
    n_wide = q_start // wide
    n_narrow = (q_start - n_wide * wide) // tq

    def wide_block(j, carry):
        block(j * wide, sub_per_wide, False)
        return carry

    lax.fori_loop(0, n_wide, wide_block, 0)
    for v in range(sub_per_wide):
        @pl.when(n_narrow == v)
        def _():
            block(n_wide * wide, v + 1, True)

    lam = _lambda_val(lam_ref, lam_init)
    inv_l = 1.0 / l_ref[...]
    acc = acc_ref[...] * inv_l
    o = acc[:tq] - lam * acc[tq:]
    o_ref[...] = (_sub_ln(o, g_ref[...], lam_init) * _silu(gz_ref[...])).astype(o_ref.dtype)


def _attn_prompt(q, k, v, gz, lam, subln, lam_init, tq):
    t, hd = q.shape
    heads = hd // LANES
    wide = min(8 * tq, t)
    body = functools.partial(_attn_body, tq=tq, wide=wide, heads=heads, lam_init=lam_init)
    return pl.pallas_call(
        body,
        grid=(heads, t // tq),
        in_specs=[pl.BlockSpec((tq, LANES), lambda h, i: (i, h)),
                  pl.BlockSpec((t, LANES), lambda h, i: (0, h)),
                  pl.BlockSpec((t, LANES), lambda h, i: (0, h)),
                  pl.BlockSpec((tq, LANES), lambda h, i: (i, h)),
                  pl.BlockSpec(lam.shape, lambda h, i: (0, 0)),
                  pl.BlockSpec((1, LANES), lambda h, i: (0, 0))],
        out_specs=pl.BlockSpec((tq, LANES), lambda h, i: (i, h)),
        out_shape=jax.ShapeDtypeStruct((t, hd), BF16),
        scratch_shapes=[pltpu.VMEM((2 * tq, LANES), BF16),
                        pltpu.VMEM((t, 2 * LANES), BF16),
                        pltpu.VMEM((2 * tq, LANES), F32),
                        pltpu.VMEM((2 * tq, LANES), F32),
                        pltpu.VMEM((2 * tq, LANES), F32)],
        compiler_params=_params(("arbitrary", "arbitrary")),
        name="diff_attn_prompt",
    )(q, k, v, gz, lam, subln.reshape(1, LANES))


TSLOT = 8
PAGES_PER_STEP = 8
PAGE_GROUPS = 4


def _paged_body(pt_ref, q_ref, *refs, heads, n_pages, t_new, lam_init):
    ck_refs = refs[:PAGES_PER_STEP]
    cv_refs = refs[PAGES_PER_STEP:2 * PAGES_PER_STEP]
    (kn_ref, vn_ref, gz_ref, lam_ref, g_ref, o_ref,
     q2_ref, b0_ref, m_ref, l_ref, acc_ref) = refs[2 * PAGES_PER_STEP:]
    j = pl.program_id(1)
    n_steps = n_pages // PAGES_PER_STEP
    ng = min(PAGE_GROUPS, heads)
    hpg = heads // ng
    hrows = 2 * TSLOT
    grows = hpg * hrows
    rows = heads * hrows
    m0 = _half_mask()
    sh = lambda n: int(math.log2(n))

    ridx = lax.broadcasted_iota(jnp.int32, (rows, 1), 0)
    row_hh = (ridx >> sh(hrows)) & (hpg - 1)
    slope = _head_slope((ridx >> sh(grows)) + ng * row_hh, heads)
    tpos = ridx & (TSLOT - 1)

    def key_layout(width):
        col = lax.broadcasted_iota(jnp.int32, (1, width), 1)
        return col >> sh(hpg), col & (hpg - 1)

    @pl.when((pl.program_id(0) == 0) & (j == 0))
    def _():
        width = PAGES_PER_STEP * PAGE * hpg
        tok, col_hh = key_layout(width)
        dist = (PAGES_PER_STEP * PAGE - tok + tpos).astype(F32)
        b0_ref[...] = jnp.where(col_hh == row_hh, -slope * dist, NEG_INF)

    @pl.when(j == 0)
    def _():
        q = q_ref[0].astype(F32) * ATTN_SCALE
        for h in range(heads):
            qh = q[:, h * LANES:(h + 1) * LANES]
            base = (h % ng) * grows + (h // ng) * hrows
            q2_ref[base:base + hrows, :] = jnp.concatenate(
                [jnp.where(m0, qh, 0.0), jnp.where(m0, 0.0, qh)], axis=0).astype(BF16)
        m_ref[...] = jnp.full(m_ref.shape, NEG_INF, F32)
        l_ref[...] = jnp.zeros(l_ref.shape, F32)
        acc_ref[...] = jnp.zeros(acc_ref.shape, F32)

    def attend(k_tiles, v_tiles, bias):
        def gather(tiles, g):
            parts = [r[0, pl.ds(t * LANES * ng + g, LANES, stride=ng), :] for r, t in tiles]
            x = jnp.concatenate(parts, axis=0) if len(parts) > 1 else parts[0]
            return x.astype(BF16)

        s = jnp.concatenate(
            [lax.dot_general(q2_ref[g * grows:(g + 1) * grows, :], gather(k_tiles, g), _NT,
                             preferred_element_type=F32) for g in range(ng)], axis=0) + bias
        tiles = [s[:, c * LANES:(c + 1) * LANES] for c in range(s.shape[1] // LANES)]
        m_prev = m_ref[...]
        m_next = jnp.maximum(m_prev, jnp.max(functools.reduce(jnp.maximum, tiles), axis=-1, keepdims=True))
        p = [jnp.exp(t - m_next) for t in tiles]
        alpha = jnp.exp(m_prev - m_next)
        l_ref[...] = alpha * l_ref[...] + jnp.sum(functools.reduce(jnp.add, p), axis=-1, keepdims=True)
        p = [t.astype(BF16) for t in p]
        p = jnp.concatenate(p, axis=1) if len(p) > 1 else p[0]
        pv = jnp.concatenate(
            [jnp.dot(p[g * grows:(g + 1) * grows, :], gather(v_tiles, g), preferred_element_type=F32)
             for g in range(ng)], axis=0)
        acc_ref[...] = alpha * acc_ref[...] + pv
        m_ref[...] = m_next

    @pl.when(j < n_steps)
    def _():
        gap = ((n_steps - 1 - j) * (PAGES_PER_STEP * PAGE)).astype(F32)
        attend([(r, t) for r in ck_refs for t in range(hpg)],
               [(r, t) for r in cv_refs for t in range(hpg)],
               b0_ref[...] - slope * gap)

    @pl.when(j == n_steps)
    def _():
        tok, col_hh = key_layout(LANES)
        valid = (col_hh == row_hh) & (tok <= tpos) & (tok < t_new)
        bias = jnp.where(valid, -slope * (tpos - tok).astype(F32), NEG_INF)
        attend([(kn_ref, 0)], [(vn_ref, 0)], bias)
        lam = _lambda_val(lam_ref, lam_init)
        acc = acc_ref[...] * (1.0 / l_ref[...])
        for h in range(heads):
            base = (h % ng) * grows + (h // ng) * hrows
            o = acc[base:base + TSLOT] - lam * acc[base + TSLOT:base + 2 * TSLOT]
            hl = slice(h * LANES, (h + 1) * LANES)
            o_ref[0, :, hl] = _sub_ln(o, g_ref[...], lam_init) * _silu(gz_ref[0, :, hl])


def _attn_paged(q, k_new, v_new, gz, cache_k, cache_v, page_table, lam, subln, lam_init, t_new):
    bsz, _, hd = q.shape
    heads = hd // LANES
    n_pages = page_table.shape[1]
    assert n_pages % PAGES_PER_STEP == 0
    n_steps = n_pages // PAGES_PER_STEP
    rows = heads * 2 * TSLOT
    ng = min(PAGE_GROUPS, heads)
    hpg = heads // ng
    assert heads == ng * hpg and hpg & (hpg - 1) == 0 and t_new <= LANES // hpg and t_new <= TSLOT
    body = functools.partial(_paged_body, heads=heads, n_pages=n_pages, t_new=t_new, lam_init=lam_init)

    def page(a):
        idx = lambda b, j, pt: (pt[b, jnp.minimum(j, n_steps - 1) * PAGES_PER_STEP + a], 0, 0)
        return pl.BlockSpec((1, PAGE * heads, LANES), idx)

    per_b = lambda b, j, pt: (b, 0, 0)
    grid_spec = pltpu.PrefetchScalarGridSpec(
        num_scalar_prefetch=1,
        grid=(bsz, n_steps + 1),
        in_specs=[pl.BlockSpec((1, TSLOT, hd), per_b),
                  *[page(a) for a in range(PAGES_PER_STEP)], *[page(a) for a in range(PAGES_PER_STEP)],
                  pl.BlockSpec((1, LANES * ng, LANES), per_b),
                  pl.BlockSpec((1, LANES * ng, LANES), per_b),
                  pl.BlockSpec((1, TSLOT, hd), per_b),
                  pl.BlockSpec(lam.shape, lambda b, j, pt: (0, 0)),
                  pl.BlockSpec((1, LANES), lambda b, j, pt: (0, 0))],
        out_specs=pl.BlockSpec((1, TSLOT, hd), per_b),
        scratch_shapes=[pltpu.VMEM((rows, LANES), BF16),
                        pltpu.VMEM((rows, PAGES_PER_STEP * PAGE * hpg), F32),
                        pltpu.VMEM((rows, LANES), F32),
                        pltpu.VMEM((rows, LANES), F32),
                        pltpu.VMEM((rows, LANES), F32)])
    return pl.pallas_call(
        body,
        grid_spec=grid_spec,
        out_shape=jax.ShapeDtypeStruct((bsz, TSLOT, hd), F32),
        compiler_params=_params(("arbitrary", "arbitrary")),
        name="diff_attn_paged",
    )(page_table, q, *([cache_k] * PAGES_PER_STEP), *([cache_v] * PAGES_PER_STEP), k_new, v_new, gz, lam,
      subln.reshape(1, LANES))


def _row_tile(m):
    return min(1024, m)


def _rwkv_layer(x, shift, scale, gate, h_prev_row, s0, p, *, batch, t_len):
    m, d = x.shape
    tm = _row_tile(m)
    tn = _col_tile(d)
    h = _modnorm(x, p["norm"], shift, scale, tm)
    h3 = h.reshape(batch, t_len, d)
    hprev = jnp.concatenate([h_prev_row[:, None, :], h3[:, :-1]], axis=1).reshape(m, d)

    rkvg, = _fused_linear("rwkv_in_proj", _mix_prologue, None, [h, hprev], [p["mu"]],
                          p["w_in"], [], [], [F32], tm=tm, tn=tn, groups=4)
    seg = WKV_CHUNK if t_len >= WKV_CHUNK else t_len
    assert t_len % seg == 0
    lw, cum, asig = _lora(h, hprev, p["mu"], p["w0"], p["w1"], p["w2"], p["a0"], p["a1"], p["a2"],
                          min(256, m), seg)

    t_pad = -(-t_len // WKV_CHUNK) * WKV_CHUNK
    if t_pad != t_len:
        def padr(a, mode="constant"):
            return jnp.pad(a.reshape(batch, t_len, -1), ((0, 0), (0, t_pad - t_len), (0, 0)),
                           mode=mode).reshape(batch * t_pad, -1)
        rkvg_w, cum_w, lw_w, asig_w = padr(rkvg), padr(cum, "edge"), padr(lw), padr(asig)
    else:
        rkvg_w, cum_w, lw_w, asig_w = rkvg, cum, lw, asig
    t_blk = min(256, t_pad)
    o_pre, s_fin = _wkv(rkvg_w, cum_w, lw_w, asig_w, p["k_k"], p["k_a"], p["r_k"], p["lnx_w"], p["lnx_b"],
                        s0, batch=batch, t_len=t_pad, t_blk=t_blk)
    if t_pad != t_len:
        o_pre = o_pre.reshape(batch, t_pad, d)[:, :t_len].reshape(m, d)

    x_new, = _fused_linear("rwkv_out_proj", None, _residual_epilogue, [o_pre], [],
                           p["w_out"], [x], [gate], [F32], tm=tm, tn=_wide_col_tile(d))
    return x_new, s_fin, h3[:, -1]


def _tile_gain(gain, n):
    return jnp.tile(gain.reshape(1, LANES), (1, n // LANES))


def _shared_kv(x, shift, scale, p):
    m, d = x.shape
    tm = _row_tile(m)
    qd = p["w_k"].shape[1]
    xn = _modnorm(x, p["norm"], shift, scale, tm, BF16)
    k, k16 = _fused_linear("kv_k_proj", None, _group_rms_epilogue(True), [xn], [], p["w_k"],
                           [], [_tile_gain(p["k_norm"], qd)], [F32, BF16], tm=tm, tn=_wide_col_tile(qd))
    v, v16 = _fused_linear("kv_v_proj", None, None, [xn], [], p["w_v"],
                           [], [], [F32, BF16], tm=tm, tn=_wide_col_tile(p["w_v"].shape[1]))
    return k, v, k16, v16


def _diff_queries(x, shift, scale, p):
    m, d = x.shape
    tm = _row_tile(m)
    qd = p["w_q"].shape[1]
    xn = _modnorm(x, p["norm"], shift, scale, tm, BF16)
    q16, = _fused_linear("diff_q_proj", None, _group_rms_epilogue(False), [xn], [], p["w_q"],
                         [], [_tile_gain(p["q_norm"], qd)], [BF16], tm=tm, tn=_wide_col_tile(qd))
    gz, = _fused_linear("diff_gate_proj", None, None, [xn], [], p["w_z"],
                        [], [], [F32], tm=tm, tn=_wide_col_tile(p["w_z"].shape[1]))
    return q16, gz


def _diff_out(x, o16, gate, w_out):
    m, d = x.shape
    y, = _fused_linear("diff_out_proj", None, _residual_epilogue, [o16], [], w_out,
                       [x], [gate], [F32], tm=_row_tile(m), tn=_wide_col_tile(d))
    return y


def kernel(x_prompt, x_sample, c_prompt, c_sample, state_wkv, state_shift, cache_k, cache_v, page_table,
           a_mod_w, a_mod_b, a_norm, a_mu, a_w_in, a_w0, a_w1, a_w2, a_a0, a_a1, a_a2,
           a_k_k, a_k_a, a_r_k, a_lnx_w, a_lnx_b, a_w_out,
           kv_mod_w, kv_mod_b, kv_norm, kv_w, kv_k_norm,
           b_mod_w, b_mod_b, b_norm, b_w_in, b_q_norm, b_lam, b_subln, b_w_out):
    bp, t_p, d = x_prompt.shape
    bs, t_s, _ = x_sample.shape
    n_a = a_mod_w.shape[0]
    n_b = b_mod_w.shape[0]
    depth = n_a + n_b
    heads_r = d // RWKV_HEAD
    q_dim = b_w_in.shape[2] - d
    lora = a_w1.shape[2]
    lora_pad = -(-lora // LANES) * LANES

    xp = x_prompt.reshape(bp * t_p, d)
    xs = x_sample.reshape(bs * t_s, d)

    n_c = bp + bs
    c_all = jnp.pad(jnp.concatenate([c_prompt, c_sample], axis=0), ((0, -n_c % 16), (0, 0)))

    def mods(w, b, n):
        mv = _mod_vectors(c_all, w, b)
        parts = jnp.split(mv, n, axis=-1)
        assert bp == 1
        prompt = [q[0:1] for q in parts]
        sample = [jnp.repeat(q[bp:n_c], t_s, axis=0) for q in parts]
        return prompt, sample

    wkv_p, wkv_s, shift_p, shift_s = [], [], [], []
    for layer in range(n_a):
        prm = dict(
            norm=a_norm[layer], mu=a_mu[layer],
            w_in=a_w_in[layer].reshape(d, 4 * d).astype(BF16),
            w0=a_w0[layer], a0=a_a0[layer],
            w1=jnp.pad(a_w1[layer], ((0, 0), (0, lora_pad - lora))).astype(BF16),
            w2=jnp.pad(a_w2[layer], ((0, lora_pad - lora), (0, 0))).astype(BF16),
            a1=jnp.pad(a_a1[layer], ((0, 0), (0, lora_pad - lora))).astype(BF16),
            a2=jnp.pad(a_a2[layer], ((0, lora_pad - lora), (0, 0))).astype(BF16),
            k_k=a_k_k[layer], k_a=a_k_a[layer], r_k=a_r_k[layer],
            lnx_w=a_lnx_w[layer], lnx_b=a_lnx_b[layer],
            w_out=a_w_out[layer].astype(BF16))
        (sh_p, sc_p, gt_p), (sh_s, sc_s, gt_s) = mods(a_mod_w[layer], a_mod_b[layer], 3)
        xp, s_fin, last = _rwkv_layer(
            xp, sh_p, sc_p, gt_p, jnp.zeros((bp, d), F32),
            jnp.zeros((bp, heads_r, RWKV_HEAD, RWKV_HEAD), F32), prm, batch=bp, t_len=t_p)
        wkv_p.append(s_fin)
        shift_p.append(last)
        xs, s_fin, last = _rwkv_layer(
            xs, sh_s, sc_s, gt_s, state_shift[layer], state_wkv[layer], prm, batch=bs, t_len=t_s)
        wkv_s.append(s_fin)
        shift_s.append(last)

    (sh_p, sc_p), (sh_s, sc_s) = mods(kv_mod_w, kv_mod_b, 2)
    kvp = dict(norm=kv_norm, w_k=kv_w[:, :q_dim].astype(BF16), w_v=kv_w[:, q_dim:].astype(BF16),
               k_norm=kv_k_norm)
    k_p, v_p, k_p16, v_p16 = _shared_kv(xp, sh_p, sc_p, kvp)
    k_s, v_s, _, _ = _shared_kv(xs, sh_s, sc_s, kvp)

    n_pool = cache_k.shape[0]
    pad_tok = lambda a, n: jnp.pad(a.reshape(bs, t_s, -1), ((0, 0), (0, n - t_s), (0, 0)))
    heads_q = q_dim // LANES
    new_tok = LANES * min(PAGE_GROUPS, heads_q) // heads_q
    as_page = lambda a: pad_tok(a, new_tok).reshape(bs, new_tok * heads_q, LANES)

    for i in range(n_b):
        layer = n_a + i
        lam_init = 0.8 - 0.6 * math.exp(-0.3 * layer)
        qp = dict(norm=b_norm[i], w_q=b_w_in[i][:, :q_dim].astype(BF16),
                  w_z=b_w_in[i][:, q_dim:].astype(BF16), q_norm=b_q_norm[i])
        (sh_p, sc_p, gt_p), (sh_s, sc_s, gt_s) = mods(b_mod_w[i], b_mod_b[i], 3)
        w_out = b_w_out[i].astype(BF16)

        q16, gz = _diff_queries(xp, sh_p, sc_p, qp)
        o16 = _attn_prompt(q16, k_p16, v_p16, gz, b_lam[i], b_subln[i], lam_init, tq=min(512, t_p))
        xp = _diff_out(xp, o16, gt_p, w_out)

        q16, gz = _diff_queries(xs, sh_s, sc_s, qp)
        o = _attn_paged(pad_tok(q16.astype(F32), TSLOT), as_page(k_s), as_page(v_s), pad_tok(gz, TSLOT),
                        cache_k.reshape(n_pool, -1, LANES), cache_v.reshape(n_pool, -1, LANES), page_table,
                        b_lam[i], b_subln[i], lam_init, t_s)
        xs = _diff_out(xs, o[:, :t_s].reshape(bs * t_s, -1).astype(BF16), gt_s, w_out)

    heads_d = q_dim // LANES
    return (xp.reshape(bp, t_p, d), xs.reshape(bs, t_s, d),
            jnp.stack(wkv_p), jnp.stack(shift_p),
            k_p.reshape(bp, t_p, heads_d, LANES), v_p.reshape(bp, t_p, heads_d, LANES),
            jnp.stack(wkv_s), jnp.stack(shift_s),
            k_s.reshape(bs, t_s, heads_d, LANES), v_s.reshape(bs, t_s, heads_d, LANES))
```

```python
import functools
import math

import jax
import jax.numpy as jnp
from jax import lax
from jax.experimental import pallas as pl
from jax.experimental.pallas import tpu as pltpu

F32 = jnp.float32
BF16 = jnp.bfloat16

LANES = 128
RWKV_HEAD = 64
DIFF_HEAD = 64
PAGE = 128
RMS_EPS = 1e-6
LNX_EPS = 64e-5
NEG_INF = -1e30
ATTN_SCALE = DIFF_HEAD ** -0.5
LOG2E = 1.4426950408889634
EPILOGUE_ROWS = 256
WKV_CHUNK = 64
WKV_PAIRS = 16
VMEM_LIMIT = 56 * 1024 * 1024

_NT = (((1,), (1,)), ((), ()))
_TN = (((0,), (0,)), ((), ()))


def _dot(a, b):
    return jnp.dot(a.astype(BF16), b.astype(BF16), preferred_element_type=F32)


def _dot_nt(a, b):
    return lax.dot_general(a.astype(BF16), b.astype(BF16), _NT, preferred_element_type=F32)


def _dot_tn(a, b):
    return lax.dot_general(a.astype(BF16), b.astype(BF16), _TN, preferred_element_type=F32)


def _silu(x):
    return x * (1.0 / (1.0 + jnp.exp(-x)))


def _col_tile(n):
    return next(t for t in (512, 384, 256, 128) if n % t == 0)


def _wide_col_tile(n):
    return next(t for t in (1024, 512, 384, 256, 128) if n % t == 0)


def _params(sem):
    return pltpu.CompilerParams(dimension_semantics=sem, vmem_limit_bytes=VMEM_LIMIT)


def _half_mask():
    return lax.broadcasted_iota(jnp.int32, (1, LANES), 1) < 64


def _group64_sum(x, m0):
    s0 = jnp.sum(jnp.where(m0, x, 0.0), axis=-1, keepdims=True)
    s1 = jnp.sum(jnp.where(m0, 0.0, x), axis=-1, keepdims=True)
    return jnp.where(m0, s0, s1)


def _mod_body(c_ref, w_ref, b_ref, o_ref):
    o_ref[...] = _dot(_silu(c_ref[...]), w_ref[...]) + b_ref[...]


def _mod_vectors(c, w, b):
    m, d = c.shape
    n = w.shape[1]
    tn = _col_tile(n)
    return pl.pallas_call(
        _mod_body,
        grid=(n // tn,),
        in_specs=[pl.BlockSpec((m, d), lambda j: (0, 0)),
                  pl.BlockSpec((d, tn), lambda j: (0, j)),
                  pl.BlockSpec((1, tn), lambda j: (0, j))],
        out_specs=pl.BlockSpec((m, tn), lambda j: (0, j)),
        out_shape=jax.ShapeDtypeStruct((m, n), F32),
        compiler_params=_params(("arbitrary",)),
        name="mod_vectors",
    )(c, w, b.reshape(1, n))


def _modnorm_val(x, g, shift, scale):
    r = lax.rsqrt(jnp.mean(x * x, axis=-1, keepdims=True) + RMS_EPS)
    return (x * r) * g * (1.0 + scale) + shift


def _modnorm_body(x_ref, g_ref, sh_ref, sc_ref, o_ref):
    o_ref[...] = _modnorm_val(x_ref[...], g_ref[...], sh_ref[...], sc_ref[...]).astype(o_ref.dtype)


def _modnorm(x, g, shift, scale, tm, out_dtype=F32):
    m, d = x.shape
    rb = shift.shape[0]
    return pl.pallas_call(
        _modnorm_body,
        grid=(m // tm,),
        in_specs=[pl.BlockSpec((tm, d), lambda i: (i, 0)),
                  pl.BlockSpec((1, d), lambda i: (0, 0)),
                  pl.BlockSpec((rb, d), lambda i: (0, 0)),
                  pl.BlockSpec((rb, d), lambda i: (0, 0))],
        out_specs=pl.BlockSpec((tm, d), lambda i: (i, 0)),
        out_shape=jax.ShapeDtypeStruct((m, d), out_dtype),
        compiler_params=_params(("arbitrary",)),
        name="modnorm",
    )(x, g.reshape(1, d), shift, scale)


def _fused_linear(name, prologue, epilogue, prow, pbc, w, erow, ebc, out_dtypes, *, tm, tn, groups=1):
    prow = [a if isinstance(a, tuple) else (a, 0) for a in prow]
    prow_blk = [cb for _, cb in prow]
    prow = [a for a, _ in prow]
    m = prow[0].shape[0]
    k, n = w.shape
    nj = n // tn
    njg = nj // groups
    rc = min(EPILOGUE_ROWS, tm)
    n_prow, n_pbc, n_erow, n_ebc, n_out = len(prow), len(pbc), len(erow), len(ebc), len(out_dtypes)

    def body(*refs):
        it = iter(refs)
        prow_r = [next(it) for _ in range(n_prow)]
        pbc_r = [next(it) for _ in range(n_pbc)]
        w_r = next(it)
        erow_r = [next(it) for _ in range(n_erow)]
        ebc_r = [next(it) for _ in range(n_ebc)]
        out_r = [next(it) for _ in range(n_out)]
        xs = next(it) if prologue is not None else prow_r[0]
        acc_s = next(it) if epilogue is not None else None
        j = pl.program_id(1)

        if prologue is not None:
            @pl.when(j % njg == 0)
            def _():
                xs[...] = prologue(prow_r, pbc_r, j // njg).astype(BF16)

        acc = jnp.dot(xs[...], w_r[...], preferred_element_type=F32)
        if epilogue is None:
            for o in out_r:
                o[...] = acc.astype(o.dtype)
            return
        acc_s[...] = acc

        def chunk(c, carry):
            rs = pl.ds(pl.multiple_of(c * rc, rc), rc)
            res = epilogue(acc_s[rs, :], [r[rs, :] for r in erow_r],
                           [r[rs, :] if r.shape[0] == tm and tm > 1 else r[...] for r in ebc_r])
            for o, r in zip(out_r, res):
                o[rs, :] = r.astype(o.dtype)
            return carry

        lax.fori_loop(0, tm // rc, chunk, 0)

    in_specs = ([pl.BlockSpec((tm, k), lambda i, j, cb=cb: (i, cb)) for cb in prow_blk]
                + [pl.BlockSpec(a.shape, lambda i, j: (0, 0)) for a in pbc]
                + [pl.BlockSpec((k, tn), lambda i, j: (0, j))]
                + [pl.BlockSpec((tm, tn), lambda i, j: (i, j)) for _ in erow]
                + [pl.BlockSpec((a.shape[0], tn), lambda i, j: (0, j)) for a in ebc])
    outs = pl.pallas_call(
        body,
        grid=(m // tm, nj),
        in_specs=in_specs,
        out_specs=[pl.BlockSpec((tm, tn), lambda i, j: (i, j)) for _ in out_dtypes],
        out_shape=[jax.ShapeDtypeStruct((m, n), dt) for dt in out_dtypes],
        scratch_shapes=([pltpu.VMEM((tm, k), BF16)] if prologue is not None else [])
                       + ([pltpu.VMEM((tm, tn), F32)] if epilogue is not None else []),
        compiler_params=_params(("arbitrary", "arbitrary")),
        name=name,
    )(*prow, *pbc, w, *erow, *ebc)
    return outs


def _residual_epilogue(acc, erows, ebcs):
    return (erows[0] + ebcs[0] * acc,)


def _group_rms_epilogue(with_bf16):
    def epilogue(acc, erows, ebcs):
        m0 = _half_mask()
        gain = ebcs[0]
        cols = []
        for s in range(acc.shape[1] // LANES):
            x = acc[:, s * LANES:(s + 1) * LANES]
            ms = _group64_sum(x * x, m0) * (1.0 / 64.0)
            cols.append(x * lax.rsqrt(ms + RMS_EPS) * gain[:, s * LANES:(s + 1) * LANES])
        y = jnp.concatenate(cols, axis=1) if len(cols) > 1 else cols[0]
        return (y, y) if with_bf16 else (y,)
    return epilogue


def _mix_prologue(prow_r, pbc_r, g):
    h = prow_r[0][...]
    mu = pbc_r[0][pl.ds(g, 1), :]
    return h + (prow_r[1][...] - h) * mu


def _lora_body(h_ref, hp_ref, mu_ref, w0_ref, w1_ref, w2_ref, a0_ref, a1_ref, a2_ref, lw_ref, cum_ref, a_ref,
               *, seg):
    h = h_ref[...]
    dx = hp_ref[...] - h
    xw = h + dx * mu_ref[4:5, :]
    xa = h + dx * mu_ref[5:6, :]
    wl = w0_ref[...] + _dot(jnp.tanh(_dot(xw, w1_ref[...])), w2_ref[...])
    z = -wl
    softplus = jnp.maximum(z, 0.0) + jnp.log(1.0 + jnp.exp(-jnp.abs(z)))
    lw = -jnp.exp(-softplus - 0.5)
    lw_ref[...] = lw
    tm = lw.shape[0]
    row = lax.broadcasted_iota(jnp.int32, (tm, tm), 0)
    col = lax.broadcasted_iota(jnp.int32, (tm, tm), 1)
    seg_shift = int(math.log2(seg))
    tril = ((row >= col) & ((row >> seg_shift) == (col >> seg_shift))).astype(BF16)
    hi = lw.astype(BF16)
    r1 = lw - hi.astype(F32)
    mid = r1.astype(BF16)
    lo = (r1 - mid.astype(F32)).astype(BF16)
    cum_ref[...] = (jnp.dot(tril, hi, preferred_element_type=F32)
                    + jnp.dot(tril, mid, preferred_element_type=F32)
                    + jnp.dot(tril, lo, preferred_element_type=F32))
    al = a0_ref[...] + _dot(_dot(xa, a1_ref[...]), a2_ref[...])
    a_ref[...] = 1.0 / (1.0 + jnp.exp(-al))


def _lora(h, hprev, mu, w0, w1, w2, a0, a1, a2, tm, seg):
    m, d = h.shape
    assert tm % seg == 0 and seg & (seg - 1) == 0
    row = pl.BlockSpec((tm, d), lambda i: (i, 0))
    full = lambda a: pl.BlockSpec(a.shape, lambda i: (0, 0))
    args = (h, hprev, mu, w0.reshape(1, d), w1, w2, a0.reshape(1, d), a1, a2)
    return pl.pallas_call(
        functools.partial(_lora_body, seg=seg),
        grid=(m // tm,),
        in_specs=[row, row] + [full(a) for a in args[2:]],
        out_specs=[row, row, row],
        out_shape=[jax.ShapeDtypeStruct((m, d), F32)] * 3,
        compiler_params=_params(("arbitrary",)),
        name="rwkv_lora",
    )(*args)


def _wkv_body(r_ref, k_ref, v_ref, g_ref, cum_ref, lw_ref, a_ref, kk_ref, ka_ref, rk_ref, lnw_ref, lnb_ref, s0_ref,
              o_ref, sf_ref, z_ref, *, chunk, n_chunks, pairs):
    c2 = 2 * chunk
    t_blk = pl.program_id(2)
    m0 = _half_mask()

    @pl.when(t_blk == 0)
    def _():
        zeros = jnp.zeros((RWKV_HEAD, RWKV_HEAD), F32)
        for pi in range(pairs):
            top = jnp.concatenate([s0_ref[0, 2 * pi], zeros], axis=1)
            bot = jnp.concatenate([zeros, s0_ref[0, 2 * pi + 1]], axis=1)
            z_ref[pi] = jnp.concatenate([top, bot], axis=0)

    row = lax.broadcasted_iota(jnp.int32, (c2, c2), 0)
    col = lax.broadcasted_iota(jnp.int32, (c2, c2), 1)
    strict = row > col
    eye = (row == col).astype(F32)
    incl2 = (lax.broadcasted_iota(jnp.int32, (c2, 2 * c2), 0)
             >= (lax.broadcasted_iota(jnp.int32, (c2, 2 * c2), 1) & (c2 - 1)))
    n_double = int(math.log2(chunk)) - 1

    def stack(x):
        return jnp.concatenate([jnp.where(m0, x, 0.0), jnp.where(m0, 0.0, x)], axis=0)

    prs = range(pairs)
    lanes = [slice(p * LANES, (p + 1) * LANES) for p in prs]

    def step(c, carry):
        sl = pl.ds(pl.multiple_of(c * chunk, chunk), chunk)
        r = [r_ref[sl, l] for l in lanes]
        kr = [k_ref[sl, l] for l in lanes]
        v = [v_ref[sl, l] for l in lanes]
        cum = [cum_ref[sl, l] for l in lanes]
        lw = [lw_ref[sl, l] for l in lanes]
        asg = [a_ref[sl, l] for l in lanes]

        kk = [kr[p] * kk_ref[:, lanes[p]] for p in prs]
        kk = [x / jnp.maximum(jnp.sqrt(_group64_sum(x * x, m0)), 1e-12) for x in kk]
        k = [kr[p] * (1.0 + (asg[p] - 1.0) * ka_ref[:, lanes[p]]) for p in prs]

        cum_end = [x[chunk - 1:chunk, :] for x in cum]
        e_out = [jnp.exp(cum_end[p] - cum[p]) for p in prs]
        a_hat = [-kk[p] * jnp.exp(cum[p] - lw[p] - cum_end[p]) for p in prs]
        r_hat = [r[p] * jnp.exp(cum[p] - cum_end[p]) for p in prs]
        b_hat = [kk[p] * asg[p] * e_out[p] for p in prs]
        k_hat = [k[p] * e_out[p] for p in prs]

        ar = [jnp.concatenate([stack(a_hat[p]), stack(r_hat[p])], axis=0) for p in prs]
        bk = [jnp.concatenate([stack(b_hat[p]), stack(k_hat[p])], axis=0) for p in prs]
        v_st = [stack(x) for x in v]

        mm = [_dot_nt(ar[p], bk[p]) for p in prs]
        l_ab = [jnp.where(strict, x[:c2, :c2], 0.0) for x in mm]
        l_ak = [jnp.where(strict, x[:c2, c2:], 0.0) for x in mm]
        l_r = [jnp.where(incl2, x[c2:, :], 0.0) for x in mm]

        t_inv = [eye + x for x in l_ab]
        l_pow = l_ab
        for _ in range(n_double):
            l_pow = [_dot(x, x) for x in l_pow]
            t_inv = [t + _dot(x, t) for x, t in zip(l_pow, t_inv)]

        lv = [_dot(l_ak[p], v_st[p]) for p in prs]
        zd = [z_ref[p] * jnp.exp(cum_end[p]) for p in prs]
        pz = [_dot_nt(ar[p], zd[p]) for p in prs]
        u = [_dot(t_inv[p], pz[p][:c2] + lv[p]) for p in prs]
        uv = [jnp.concatenate([u[p], v_st[p]], axis=0) for p in prs]
        y_st = [pz[p][c2:] + _dot(l_r[p], uv[p]) for p in prs]
        for p in prs:
            z_ref[p] = zd[p] + _dot_tn(uv[p], bk[p])

        for p in prs:
            y = y_st[p][:chunk] + y_st[p][chunk:]
            mean = _group64_sum(y, m0) * (1.0 / 64.0)
            yc = y - mean
            var = _group64_sum(yc * yc, m0) * (1.0 / 64.0)
            bonus = _group64_sum(r[p] * k[p] * rk_ref[:, lanes[p]], m0) * v[p]
            o = yc * lax.rsqrt(var + LNX_EPS) * lnw_ref[:, lanes[p]] + lnb_ref[:, lanes[p]] + bonus
            o_ref[sl, lanes[p]] = (o * _silu(g_ref[sl, lanes[p]])).astype(o_ref.dtype)
        return carry

    lax.fori_loop(0, n_chunks, step, 0)

    @pl.when(t_blk == pl.num_programs(2) - 1)
    def _():
        for pi in range(pairs):
            z = z_ref[pi]
            sf_ref[0, 2 * pi] = z[:RWKV_HEAD, :RWKV_HEAD]
            sf_ref[0, 2 * pi + 1] = z[RWKV_HEAD:, RWKV_HEAD:]


def _wkv(rkvg, cum, lw, asig, k_k, k_a, r_k, lnx_w, lnx_b, s0, *, batch, t_len, t_blk):
    m, d = lw.shape
    pairs = next(p for p in (WKV_PAIRS, 8, 4, 2, 1) if (d // LANES) % p == 0)
    width = pairs * LANES
    hp = d // width
    nt = t_len // t_blk
    heads = d // RWKV_HEAD
    vec = lambda a: a.reshape(1, d)
    row = lambda off: pl.BlockSpec((t_blk, width), lambda b, p, t, off=off: (b * nt + t, off + p))
    par = pl.BlockSpec((1, width), lambda b, p, t: (0, p))
    st = pl.BlockSpec((1, 2 * pairs, RWKV_HEAD, RWKV_HEAD), lambda b, p, t: (b, p, 0, 0))
    body = functools.partial(_wkv_body, chunk=WKV_CHUNK, n_chunks=t_blk // WKV_CHUNK, pairs=pairs)
    return pl.pallas_call(
        body,
        grid=(batch, hp, nt),
        in_specs=[row(0), row(hp), row(2 * hp), row(3 * hp), row(0), row(0), row(0),
                  par, par, par, par, par, st],
        out_specs=[row(0), st],
        out_shape=[jax.ShapeDtypeStruct((m, d), BF16),
                   jax.ShapeDtypeStruct((batch, heads, RWKV_HEAD, RWKV_HEAD), F32)],
        scratch_shapes=[pltpu.VMEM((pairs, LANES, LANES), F32)],
        compiler_params=_params(("arbitrary", "arbitrary", "arbitrary")),
        name="wkv7_chunked",
    )(rkvg, rkvg, rkvg, rkvg, cum, lw, asig, vec(k_k), vec(k_a), vec(r_k), vec(lnx_w), vec(lnx_b), s0)


def _lambda_val(lam_ref, lam_init):
    lq = lam_ref[...]
    s1 = jnp.sum(lq[0:1] * lq[1:2], axis=-1, keepdims=True)
    s2 = jnp.sum(lq[2:3] * lq[3:4], axis=-1, keepdims=True)
    return jnp.exp(s1) - jnp.exp(s2) + lam_init


def _sub_ln(o, g, lam_init):
    return o * lax.rsqrt(jnp.mean(o * o, axis=-1, keepdims=True) + RMS_EPS) * g * (1.0 - lam_init)


def _head_slope(h, heads):
    return jnp.exp2((-8.0 / heads) * (h + 1).astype(F32))


def _online_softmax_step(s, v_aug, m_ref, l_ref, acc_ref):
    tiles = [s[:, c * LANES:(c + 1) * LANES] for c in range(s.shape[1] // LANES)]
    mt = functools.reduce(jnp.maximum, tiles)
    m_prev = m_ref[...]
    m_next = jnp.maximum(m_prev, jnp.max(mt, axis=-1, keepdims=True))
    p = [jnp.exp2(t - m_next).astype(BF16) for t in tiles]
    p = jnp.concatenate(p, axis=1) if len(p) > 1 else p[0]
    alpha = jnp.exp2(m_prev - m_next)
    pv = jnp.dot(p, v_aug, preferred_element_type=F32)
    acc_ref[...] = alpha * acc_ref[...] + pv[:, :LANES]
    l_ref[...] = alpha * l_ref[...] + pv[:, LANES:]
    m_ref[...] = m_next


def _attn_body(q_ref, k_ref, v_ref, gz_ref, lam_ref, g_ref, o_ref, q2_ref, va_ref, cm_ref, m_ref, l_ref, acc_ref, *,
               tq, wide, heads, lam_init):
    h = pl.program_id(0)
    i = pl.program_id(1)
    m0 = _half_mask()
    slope = _head_slope(jnp.full((1, 1), h, jnp.int32), heads) * LOG2E

    @pl.when(i == 0)
    def _():
        va_ref[:, :LANES] = v_ref[...]
        va_ref[:, LANES:] = jnp.ones(v_ref.shape, BF16)

    @pl.when((h == 0) & (i == 0))
    def _():
        rr = lax.broadcasted_iota(jnp.int32, (2 * tq, tq), 0) & (tq - 1)
        cc = lax.broadcasted_iota(jnp.int32, (2 * tq, tq), 1)
        cm_ref[...] = jnp.where(cc <= rr, 0.0, NEG_INF)

    q = q_ref[...].astype(F32) * (ATTN_SCALE * LOG2E)
    q2_ref[...] = jnp.concatenate([jnp.where(m0, q, 0.0), jnp.where(m0, 0.0, q)], axis=0).astype(BF16)
    m_ref[...] = jnp.full(m_ref.shape, NEG_INF, F32)
    l_ref[...] = jnp.zeros(l_ref.shape, F32)
    acc_ref[...] = jnp.zeros(acc_ref.shape, F32)

    q_start = i * tq

    def block(k_start, n_sub, diagonal_last):
        sls = [pl.ds(pl.multiple_of(k_start + c * tq, tq), tq) for c in range(n_sub)]
        col = lax.broadcasted_iota(jnp.int32, (1, tq), 1)
        ss = [lax.dot_general(q2_ref[...], k_ref[sl, :], _NT, preferred_element_type=F32) for sl in sls]
        for c in range(n_sub):
            s = ss[c] + slope * (k_start + c * tq - q_start + col).astype(F32)
            if diagonal_last and c == n_sub - 1:
                s = s + cm_ref[...]
            _online_softmax_step(s, va_ref[sls[c], :], m_ref, l_ref, acc_ref)

    sub_per_wide = wide // tq
    n_wide = q_start // wide
    n_narrow = (q_start - n_wide * wide) // tq

    def wide_block(j, carry):
        block(j * wide, sub_per_wide, False)
        return carry

    lax.fori_loop(0, n_wide, wide_block, 0)
    for v in range(sub_per_wide):
        @pl.when(n_narrow == v)
        def _():
            block(n_wide * wide, v + 1, True)

    lam = _lambda_val(lam_ref, lam_init)
    inv_l = 1.0 / l_ref[...]
    acc = acc_ref[...] * inv_l
    o = acc[:tq] - lam * acc[tq:]
    o_ref[...] = (_sub_ln(o, g_ref[...], lam_init) * _silu(gz_ref[...])).astype(o_ref.dtype)


def _attn_prompt(q, k, v, gz, lam, subln, lam_init, tq):
    t, hd = q.shape
    heads = hd // LANES
    wide = min(8 * tq, t)
    body = functools.partial(_attn_body, tq=tq, wide=wide, heads=heads, lam_init=lam_init)
    return pl.pallas_call(
        body,
        grid=(heads, t // tq),
        in_specs=[pl.BlockSpec((tq, LANES), lambda h, i: (i, h)),
                  pl.BlockSpec((t, LANES), lambda h, i: (0, h)),
                  pl.BlockSpec((t, LANES), lambda h, i: (0, h)),
                  pl.BlockSpec((tq, LANES), lambda h, i: (i, h)),
                  pl.BlockSpec(lam.shape, lambda h, i: (0, 0)),
                  pl.BlockSpec((1, LANES), lambda h, i: (0, 0))],
        out_specs=pl.BlockSpec((tq, LANES), lambda h, i: (i, h)),
        out_shape=jax.ShapeDtypeStruct((t, hd), BF16),
        scratch_shapes=[pltpu.VMEM((2 * tq, LANES), BF16),
                        pltpu.VMEM((t, 2 * LANES), BF16),
                        pltpu.VMEM((2 * tq, tq), F32),
                        pltpu.VMEM((2 * tq, LANES), F32),
                        pltpu.VMEM((2 * tq, LANES), F32),
                        pltpu.VMEM((2 * tq, LANES), F32)],
        compiler_params=_params(("arbitrary", "arbitrary")),
        name="diff_attn_prompt",
    )(q, k, v, gz, lam, subln.reshape(1, LANES))


TSLOT = 8
PAGES_PER_STEP = 8
PAGE_GROUPS = 4


def _paged_body(pt_ref, q_ref, *refs, heads, n_pages, t_new, lam_init):
    ck_refs = refs[:PAGES_PER_STEP]
    cv_refs = refs[PAGES_PER_STEP:2 * PAGES_PER_STEP]
    (kn_ref, vn_ref, gz_ref, lam_ref, g_ref, o_ref,
     q2_ref, b0_ref, m_ref, l_ref, acc_ref) = refs[2 * PAGES_PER_STEP:]
    j = pl.program_id(1)
    n_steps = n_pages // PAGES_PER_STEP
    ng = min(PAGE_GROUPS, heads)
    hpg = heads // ng
    hrows = 2 * TSLOT
    grows = hpg * hrows
    rows = heads * hrows
    m0 = _half_mask()
    sh = lambda n: int(math.log2(n))

    ridx = lax.broadcasted_iota(jnp.int32, (rows, 1), 0)
    row_hh = (ridx >> sh(hrows)) & (hpg - 1)
    slope = _head_slope((ridx >> sh(grows)) + ng * row_hh, heads)
    tpos = ridx & (TSLOT - 1)

    def key_layout(width):
        col = lax.broadcasted_iota(jnp.int32, (1, width), 1)
        return col >> sh(hpg), col & (hpg - 1)

    @pl.when((pl.program_id(0) == 0) & (j == 0))
    def _():
        width = PAGES_PER_STEP * PAGE * hpg
        tok, col_hh = key_layout(width)
        dist = (PAGES_PER_STEP * PAGE - tok + tpos).astype(F32)
        b0_ref[...] = jnp.where(col_hh == row_hh, -slope * dist, NEG_INF)

    @pl.when(j == 0)
    def _():
        q = q_ref[0].astype(F32) * ATTN_SCALE
        for h in range(heads):
            qh = q[:, h * LANES:(h + 1) * LANES]
            base = (h % ng) * grows + (h // ng) * hrows
            q2_ref[base:base + hrows, :] = jnp.concatenate(
                [jnp.where(m0, qh, 0.0), jnp.where(m0, 0.0, qh)], axis=0).astype(BF16)
        m_ref[...] = jnp.full(m_ref.shape, NEG_INF, F32)
        l_ref[...] = jnp.zeros(l_ref.shape, F32)
        acc_ref[...] = jnp.zeros(acc_ref.shape, F32)

    def attend(k_tiles, v_tiles, bias):
        def gather(tiles, g):
            parts = [r[0, pl.ds(t * LANES * ng + g, LANES, stride=ng), :] for r, t in tiles]
            x = jnp.concatenate(parts, axis=0) if len(parts) > 1 else parts[0]
            return x.astype(BF16)

        s = jnp.concatenate(
            [lax.dot_general(q2_ref[g * grows:(g + 1) * grows, :], gather(k_tiles, g), _NT,
                             preferred_element_type=F32) for g in range(ng)], axis=0) + bias
        tiles = [s[:, c * LANES:(c + 1) * LANES] for c in range(s.shape[1] // LANES)]
        m_prev = m_ref[...]
        m_next = jnp.maximum(m_prev, jnp.max(functools.reduce(jnp.maximum, tiles), axis=-1, keepdims=True))
        p = [jnp.exp(t - m_next) for t in tiles]
        alpha = jnp.exp(m_prev - m_next)
        l_ref[...] = alpha * l_ref[...] + jnp.sum(functools.reduce(jnp.add, p), axis=-1, keepdims=True)
        p = [t.astype(BF16) for t in p]
        p = jnp.concatenate(p, axis=1) if len(p) > 1 else p[0]
        pv = jnp.concatenate(
            [jnp.dot(p[g * grows:(g + 1) * grows, :], gather(v_tiles, g), preferred_element_type=F32)
             for g in range(ng)], axis=0)
        acc_ref[...] = alpha * acc_ref[...] + pv
        m_ref[...] = m_next

    @pl.when(j < n_steps)
    def _():
        gap = ((n_steps - 1 - j) * (PAGES_PER_STEP * PAGE)).astype(F32)
        attend([(r, t) for r in ck_refs for t in range(hpg)],
               [(r, t) for r in cv_refs for t in range(hpg)],
               b0_ref[...] - slope * gap)

    @pl.when(j == n_steps)
    def _():
        tok, col_hh = key_layout(LANES)
        valid = (col_hh == row_hh) & (tok <= tpos) & (tok < t_new)
        bias = jnp.where(valid, -slope * (tpos - tok).astype(F32), NEG_INF)
        attend([(kn_ref, 0)], [(vn_ref, 0)], bias)
        lam = _lambda_val(lam_ref, lam_init)
        acc = acc_ref[...] * (1.0 / l_ref[...])
        for h in range(heads):
            base = (h % ng) * grows + (h // ng) * hrows
            o = acc[base:base + TSLOT] - lam * acc[base + TSLOT:base + 2 * TSLOT]
            hl = slice(h * LANES, (h + 1) * LANES)
            o_ref[0, :, hl] = _sub_ln(o, g_ref[...], lam_init) * _silu(gz_ref[0, :, hl])


def _attn_paged(q, k_new, v_new, gz, cache_k, cache_v, page_table, lam, subln, lam_init, t_new):
    bsz, _, hd = q.shape
    heads = hd // LANES
    n_pages = page_table.shape[1]
    assert n_pages % PAGES_PER_STEP == 0
    n_steps = n_pages // PAGES_PER_STEP
    rows = heads * 2 * TSLOT
    ng = min(PAGE_GROUPS, heads)
    hpg = heads // ng
    assert heads == ng * hpg and hpg & (hpg - 1) == 0 and t_new <= LANES // hpg and t_new <= TSLOT
    body = functools.partial(_paged_body, heads=heads, n_pages=n_pages, t_new=t_new, lam_init=lam_init)

    def page(a):
        idx = lambda b, j, pt: (pt[b, jnp.minimum(j, n_steps - 1) * PAGES_PER_STEP + a], 0, 0)
        return pl.BlockSpec((1, PAGE * heads, LANES), idx)

    per_b = lambda b, j, pt: (b, 0, 0)
    grid_spec = pltpu.PrefetchScalarGridSpec(
        num_scalar_prefetch=1,
        grid=(bsz, n_steps + 1),
        in_specs=[pl.BlockSpec((1, TSLOT, hd), per_b),
                  *[page(a) for a in range(PAGES_PER_STEP)], *[page(a) for a in range(PAGES_PER_STEP)],
                  pl.BlockSpec((1, LANES * ng, LANES), per_b),
                  pl.BlockSpec((1, LANES * ng, LANES), per_b),
                  pl.BlockSpec((1, TSLOT, hd), per_b),
                  pl.BlockSpec(lam.shape, lambda b, j, pt: (0, 0)),
                  pl.BlockSpec((1, LANES), lambda b, j, pt: (0, 0))],
        out_specs=pl.BlockSpec((1, TSLOT, hd), per_b),
        scratch_shapes=[pltpu.VMEM((rows, LANES), BF16),
                        pltpu.VMEM((rows, PAGES_PER_STEP * PAGE * hpg), F32),
                        pltpu.VMEM((rows, LANES), F32),
                        pltpu.VMEM((rows, LANES), F32),
                        pltpu.VMEM((rows, LANES), F32)])
    return pl.pallas_call(
        body,
        grid_spec=grid_spec,
        out_shape=jax.ShapeDtypeStruct((bsz, TSLOT, hd), F32),
        compiler_params=_params(("arbitrary", "arbitrary")),
        name="diff_attn_paged",
    )(page_table, q, *([cache_k] * PAGES_PER_STEP), *([cache_v] * PAGES_PER_STEP), k_new, v_new, gz, lam,
      subln.reshape(1, LANES))


def _row_tile(m):
    return min(1024, m)


def _rwkv_layer(x, shift, scale, gate, h_prev_row, s0, p, *, batch, t_len):
    m, d = x.shape
    tm = _row_tile(m)
    tn = _col_tile(d)
    h = _modnorm(x, p["norm"], shift, scale, tm)
    h3 = h.reshape(batch, t_len, d)
    hprev = jnp.concatenate([h_prev_row[:, None, :], h3[:, :-1]], axis=1).reshape(m, d)

    rkvg, = _fused_linear("rwkv_in_proj", _mix_prologue, None, [h, hprev], [p["mu"]],
                          p["w_in"], [], [], [F32], tm=tm, tn=tn, groups=4)
    seg = WKV_CHUNK if t_len >= WKV_CHUNK else t_len
    assert t_len % seg == 0
    lw, cum, asig = _lora(h, hprev, p["mu"], p["w0"], p["w1"], p["w2"], p["a0"], p["a1"], p["a2"],
                          min(256, m), seg)

    t_pad = -(-t_len // WKV_CHUNK) * WKV_CHUNK
    if t_pad != t_len:
        def padr(a, mode="constant"):
            return jnp.pad(a.reshape(batch, t_len, -1), ((0, 0), (0, t_pad - t_len), (0, 0)),
                           mode=mode).reshape(batch * t_pad, -1)
        rkvg_w, cum_w, lw_w, asig_w = padr(rkvg), padr(cum, "edge"), padr(lw), padr(asig)
    else:
        rkvg_w, cum_w, lw_w, asig_w = rkvg, cum, lw, asig
    t_blk = min(256, t_pad)
    o_pre, s_fin = _wkv(rkvg_w, cum_w, lw_w, asig_w, p["k_k"], p["k_a"], p["r_k"], p["lnx_w"], p["lnx_b"],
                        s0, batch=batch, t_len=t_pad, t_blk=t_blk)
    if t_pad != t_len:
        o_pre = o_pre.reshape(batch, t_pad, d)[:, :t_len].reshape(m, d)

    x_new, = _fused_linear("rwkv_out_proj", None, _residual_epilogue, [o_pre], [],
                           p["w_out"], [x], [gate], [F32], tm=tm, tn=_wide_col_tile(d))
    return x_new, s_fin, h3[:, -1]


def _tile_gain(gain, n):
    return jnp.tile(gain.reshape(1, LANES), (1, n // LANES))


def _shared_kv(x, shift, scale, p):
    m, d = x.shape
    tm = _row_tile(m)
    qd = p["w_k"].shape[1]
    xn = _modnorm(x, p["norm"], shift, scale, tm, BF16)
    k, k16 = _fused_linear("kv_k_proj", None, _group_rms_epilogue(True), [xn], [], p["w_k"],
                           [], [_tile_gain(p["k_norm"], qd)], [F32, BF16], tm=tm, tn=_wide_col_tile(qd))
    v, v16 = _fused_linear("kv_v_proj", None, None, [xn], [], p["w_v"],
                           [], [], [F32, BF16], tm=tm, tn=_wide_col_tile(p["w_v"].shape[1]))
    return k, v, k16, v16


def _diff_queries(x, shift, scale, p):
    m, d = x.shape
    tm = _row_tile(m)
    qd = p["w_q"].shape[1]
    xn = _modnorm(x, p["norm"], shift, scale, tm, BF16)
    q16, = _fused_linear("diff_q_proj", None, _group_rms_epilogue(False), [xn], [], p["w_q"],
                         [], [_tile_gain(p["q_norm"], qd)], [BF16], tm=tm, tn=_wide_col_tile(qd))
    gz, = _fused_linear("diff_gate_proj", None, None, [xn], [], p["w_z"],
                        [], [], [F32], tm=tm, tn=_wide_col_tile(p["w_z"].shape[1]))
    return q16, gz


def _diff_out(x, o16, gate, w_out):
    m, d = x.shape
    y, = _fused_linear("diff_out_proj", None, _residual_epilogue, [o16], [], w_out,
                       [x], [gate], [F32], tm=_row_tile(m), tn=_wide_col_tile(d))
    return y


def kernel(x_prompt, x_sample, c_prompt, c_sample, state_wkv, state_shift, cache_k, cache_v, page_table,
           a_mod_w, a_mod_b, a_norm, a_mu, a_w_in, a_w0, a_w1, a_w2, a_a0, a_a1, a_a2,
           a_k_k, a_k_a, a_r_k, a_lnx_w, a_lnx_b, a_w_out,
           kv_mod_w, kv_mod_b, kv_norm, kv_w, kv_k_norm,
           b_mod_w, b_mod_b, b_norm, b_w_in, b_q_norm, b_lam, b_subln, b_w_out):
    bp, t_p, d = x_prompt.shape
    bs, t_s, _ = x_sample.shape
    n_a = a_mod_w.shape[0]
    n_b = b_mod_w.shape[0]
    depth = n_a + n_b
    heads_r = d // RWKV_HEAD
    q_dim = b_w_in.shape[2] - d
    lora = a_w1.shape[2]
    lora_pad = -(-lora // LANES) * LANES

    xp = x_prompt.reshape(bp * t_p, d)
    xs = x_sample.reshape(bs * t_s, d)

    n_c = bp + bs
    c_all = jnp.pad(jnp.concatenate([c_prompt, c_sample], axis=0), ((0, -n_c % 16), (0, 0)))

    def mods(w, b, n):
        mv = _mod_vectors(c_all, w, b)
        parts = jnp.split(mv, n, axis=-1)
        assert bp == 1
        prompt = [q[0:1] for q in parts]
        sample = [jnp.repeat(q[bp:n_c], t_s, axis=0) for q in parts]
        return prompt, sample

    wkv_p, wkv_s, shift_p, shift_s = [], [], [], []
    for layer in range(n_a):
        prm = dict(
            norm=a_norm[layer], mu=a_mu[layer],
            w_in=a_w_in[layer].reshape(d, 4 * d).astype(BF16),
            w0=a_w0[layer], a0=a_a0[layer],
            w1=jnp.pad(a_w1[layer], ((0, 0), (0, lora_pad - lora))).astype(BF16),
            w2=jnp.pad(a_w2[layer], ((0, lora_pad - lora), (0, 0))).astype(BF16),
            a1=jnp.pad(a_a1[layer], ((0, 0), (0, lora_pad - lora))).astype(BF16),
            a2=jnp.pad(a_a2[layer], ((0, lora_pad - lora), (0, 0))).astype(BF16),
            k_k=a_k_k[layer], k_a=a_k_a[layer], r_k=a_r_k[layer],
            lnx_w=a_lnx_w[layer], lnx_b=a_lnx_b[layer],
            w_out=a_w_out[layer].astype(BF16))
        (sh_p, sc_p, gt_p), (sh_s, sc_s, gt_s) = mods(a_mod_w[layer], a_mod_b[layer], 3)
        xp, s_fin, last = _rwkv_layer(
            xp, sh_p, sc_p, gt_p, jnp.zeros((bp, d), F32),
            jnp.zeros((bp, heads_r, RWKV_HEAD, RWKV_HEAD), F32), prm, batch=bp, t_len=t_p)
        wkv_p.append(s_fin)
        shift_p.append(last)
        xs, s_fin, last = _rwkv_layer(
            xs, sh_s, sc_s, gt_s, state_shift[layer], state_wkv[layer], prm, batch=bs, t_len=t_s)
        wkv_s.append(s_fin)
        shift_s.append(last)

    (sh_p, sc_p), (sh_s, sc_s) = mods(kv_mod_w, kv_mod_b, 2)
    kvp = dict(norm=kv_norm, w_k=kv_w[:, :q_dim].astype(BF16), w_v=kv_w[:, q_dim:].astype(BF16),
               k_norm=kv_k_norm)
    k_p, v_p, k_p16, v_p16 = _shared_kv(xp, sh_p, sc_p, kvp)
    k_s, v_s, _, _ = _shared_kv(xs, sh_s, sc_s, kvp)

    n_pool = cache_k.shape[0]
    pad_tok = lambda a, n: jnp.pad(a.reshape(bs, t_s, -1), ((0, 0), (0, n - t_s), (0, 0)))
    heads_q = q_dim // LANES
    new_tok = LANES * min(PAGE_GROUPS, heads_q) // heads_q
    as_page = lambda a: pad_tok(a, new_tok).reshape(bs, new_tok * heads_q, LANES)

    for i in range(n_b):
        layer = n_a + i
        lam_init = 0.8 - 0.6 * math.exp(-0.3 * layer)
        qp = dict(norm=b_norm[i], w_q=b_w_in[i][:, :q_dim].astype(BF16),
                  w_z=b_w_in[i][:, q_dim:].astype(BF16), q_norm=b_q_norm[i])
        (sh_p, sc_p, gt_p), (sh_s, sc_s, gt_s) = mods(b_mod_w[i], b_mod_b[i], 3)
        w_out = b_w_out[i].astype(BF16)

        q16, gz = _diff_queries(xp, sh_p, sc_p, qp)
        o16 = _attn_prompt(q16, k_p16, v_p16, gz, b_lam[i], b_subln[i], lam_init, tq=min(512, t_p))
        xp = _diff_out(xp, o16, gt_p, w_out)

        q16, gz = _diff_queries(xs, sh_s, sc_s, qp)
        o = _attn_paged(pad_tok(q16.astype(F32), TSLOT), as_page(k_s), as_page(v_s), pad_tok(gz, TSLOT),
                        cache_k.reshape(n_pool, -1, LANES), cache_v.reshape(n_pool, -1, LANES), page_table,
                        b_lam[i], b_subln[i], lam_init, t_s)
        xs = _diff_out(xs, o[:, :t_s].reshape(bs * t_s, -1).astype(BF16), gt_s, w_out)

    heads_d = q_dim // LANES
    return (xp.reshape(bp, t_p, d), xs.reshape(bs, t_s, d),
            jnp.stack(wkv_p), jnp.stack(shift_p),
            k_p.reshape(bp, t_p, heads_d, LANES), v_p.reshape(bp, t_p, heads_d, LANES),
            jnp.stack(wkv_s), jnp.stack(shift_s),
            k_s.reshape(bs, t_s, heads_d, LANES), v_s.reshape(bs, t_s, heads_d, LANES))
```
